```python
import jax, jax.numpy as jnp
from jax import lax
import numpy as np

D_MODEL = 1024
BATCH = 16
SEQ = 2048
DEPTH = 4

N_MIXERS = 2
N_META = 16
N_HEADS = 16
QK_NOPE_DIM = 64
QK_ROPE_DIM = 32
QK_HEAD_DIM = QK_NOPE_DIM + QK_ROPE_DIM
V_HEAD_DIM = 64
Q_LORA_RANK = 384
KV_LORA_RANK = 256
ROPE_THETA = 10000.0
Q_BLOCK = 128
POOL_WINDOWS = (2, 4, 8, 16)
N_POOL_GROUPS = len(POOL_WINDOWS)
POOL_GROUP_DIM = D_MODEL // N_POOL_GROUPS
D_FF = 2816
CONV_WIDTH = 3
NORM_EPS = 1e-6
N_MLA_LAYERS = len(range(0, DEPTH, N_MIXERS))
N_POOL_LAYERS = DEPTH - N_MLA_LAYERS

kernel_name = "hybrid_mla_multiscale_pool_convffn"


def rmsnorm(x, g):
    xf = x.astype(jnp.float32)
    y = xf * lax.rsqrt(jnp.mean(xf * xf, axis=-1, keepdims=True) + NORM_EPS)
    return (y * g.astype(jnp.float32)).astype(x.dtype)


def rope_tables(length):
    inv = 1.0 / (ROPE_THETA ** (jnp.arange(0, QK_ROPE_DIM, 2, dtype=jnp.float32) / QK_ROPE_DIM))
    ang = jnp.arange(length, dtype=jnp.float32)[:, None] * inv[None, :]
    return jnp.cos(ang), jnp.sin(ang)


def apply_rope(x, cos, sin):
    xf = x.astype(jnp.float32)
    x1, x2 = jnp.split(xf, 2, axis=-1)
    c = cos[None, :, None, :]
    s = sin[None, :, None, :]
    return jnp.concatenate([x1 * c - x2 * s, x2 * c + x1 * s], axis=-1).astype(x.dtype)


def mla_mixer(h, w_dqkv, q_norm, w_uq, kv_norm, w_ukv, w_o, cos, sin):
    B, L, _ = h.shape
    a = h @ w_dqkv
    c_q, c_kv, k_rope = jnp.split(a, [Q_LORA_RANK, Q_LORA_RANK + KV_LORA_RANK], axis=-1)
    c_q = rmsnorm(c_q, q_norm)
    c_kv = rmsnorm(c_kv, kv_norm)
    q = (c_q @ w_uq).reshape(B, L, N_HEADS, QK_HEAD_DIM)
    q = jnp.concatenate([q[..., :QK_NOPE_DIM], apply_rope(q[..., QK_NOPE_DIM:], cos, sin)], axis=-1)
    kv = (c_kv @ w_ukv).reshape(B, L, N_HEADS, QK_NOPE_DIM + V_HEAD_DIM)
    k_nope, v = jnp.split(kv, [QK_NOPE_DIM], axis=-1)
    k_rope = apply_rope(k_rope[:, :, None, :], cos, sin)
    k = jnp.concatenate([k_nope, jnp.broadcast_to(k_rope, (B, L, N_HEADS, QK_ROPE_DIM))], axis=-1)
    scale = QK_HEAD_DIM ** -0.5
    outs = []
    for start in range(0, L, Q_BLOCK):
        end = start + Q_BLOCK
        qb = q[:, start:end]
        kb = k[:, :end]
        vb = v[:, :end]
        s = jnp.einsum('bqhd,bkhd->bhqk', qb, kb).astype(jnp.float32) * scale
        mask = jnp.arange(start, end)[:, None] >= jnp.arange(end)[None, :]
        s = jnp.where(mask[None, None], s, -jnp.inf)
        p = jax.nn.softmax(s, axis=-1).astype(vb.dtype)
        outs.append(jnp.einsum('bhqk,bkhd->bqhd', p, vb))
    o = jnp.concatenate(outs, axis=1).reshape(B, L, N_HEADS * V_HEAD_DIM)
    return o @ w_o


def pool_mixer(h, w_group, scale):
    B, L, D = h.shape
    hf = h.astype(jnp.float32).reshape(B, L, N_POOL_GROUPS, POOL_GROUP_DIM)
    csum = jnp.cumsum(hf, axis=1)
    t = jnp.arange(1, L + 1, dtype=jnp.float32)
    means = []
    for g, w in enumerate(POOL_WINDOWS):
        cg = csum[:, :, g, :]
        prev = jnp.pad(cg[:, :L - w], ((0, 0), (w, 0), (0, 0)))
        cnt = jnp.minimum(t, float(w))[None, :, None]
        means.append((cg - prev) / cnt)
    pooled = jnp.stack(means, axis=2)
    mixed = (pooled - hf).astype(h.dtype)
    y = jnp.einsum('blgc,gcd->blgd', mixed, w_group).reshape(B, L, D)
    return y * scale


def conv_ffn(h, w_up, conv_w, conv_b, w_down):
    L = h.shape[1]
    u = h @ w_up
    up = jnp.pad(u, ((0, 0), (CONV_WIDTH - 1, 0), (0, 0)))
    u = conv_b + sum(conv_w[j] * up[:, j:j + L] for j in range(CONV_WIDTH))
    gate, val = jnp.split(u, 2, axis=-1)
    return (jax.nn.silu(gate) * val) @ w_down


def setup_inputs(seed: int = 0) -> dict:
    key = jax.random.key(seed)
    ks = jax.random.split(key, 24)
    f32 = jnp.float32

    def nrm(k, shape, fan_in):
        return jax.random.normal(k, shape, f32) * (fan_in ** -0.5)

    def gain(k, shape, s=0.05):
        return 1.0 + s * jax.random.normal(k, shape, f32)

    return {
        "x": jax.random.normal(ks[0], (BATCH, SEQ, D_MODEL), f32),
        "meta_tokens": jax.random.normal(ks[1], (N_META, D_MODEL), f32),
        "norm_mix_pre": gain(ks[2], (DEPTH, D_MODEL)),
        "norm_mix_post": gain(ks[3], (DEPTH, D_MODEL)),
        "norm_ffn_pre": gain(ks[4], (DEPTH, D_MODEL)),
        "norm_ffn_post": gain(ks[5], (DEPTH, D_MODEL)),
        "mla_w_dqkv": nrm(ks[6], (N_MLA_LAYERS, D_MODEL, Q_LORA_RANK + KV_LORA_RANK + QK_ROPE_DIM), D_MODEL),
        "mla_q_norm": gain(ks[7], (N_MLA_LAYERS, Q_LORA_RANK)),
        "mla_w_uq": nrm(ks[8], (N_MLA_LAYERS, Q_LORA_RANK, N_HEADS * QK_HEAD_DIM), Q_LORA_RANK),
        "mla_kv_norm": gain(ks[9], (N_MLA_LAYERS, KV_LORA_RANK)),
        "mla_w_ukv": nrm(ks[10], (N_MLA_LAYERS, KV_LORA_RANK, N_HEADS * (QK_NOPE_DIM + V_HEAD_DIM)), KV_LORA_RANK),
        "mla_w_o": nrm(ks[11], (N_MLA_LAYERS, N_HEADS * V_HEAD_DIM, D_MODEL), N_HEADS * V_HEAD_DIM),
        "pool_w_group": nrm(ks[12], (N_POOL_LAYERS, N_POOL_GROUPS, POOL_GROUP_DIM, POOL_GROUP_DIM), POOL_GROUP_DIM),
        "pool_scale": gain(ks[13], (N_POOL_LAYERS, D_MODEL), 0.1),
        "ffn_w_up": nrm(ks[14], (DEPTH, D_MODEL, 2 * D_FF), D_MODEL),
        "ffn_conv_w": nrm(ks[15], (DEPTH, CONV_WIDTH, 2 * D_FF), CONV_WIDTH),
        "ffn_conv_b": 0.01 * jax.random.normal(ks[16], (DEPTH, 2 * D_FF), f32),
        "ffn_w_down": nrm(ks[17], (DEPTH, D_FF, D_MODEL), D_FF),
    }


def reference(x, meta_tokens, norm_mix_pre, norm_mix_post, norm_ffn_pre, norm_ffn_post,
              mla_w_dqkv, mla_q_norm, mla_w_uq, mla_kv_norm, mla_w_ukv, mla_w_o,
              pool_w_group, pool_scale, ffn_w_up, ffn_conv_w, ffn_conv_b, ffn_w_down):
    B, S, D = x.shape
    L = N_META + S
    L_pad = -(-L // Q_BLOCK) * Q_BLOCK
    meta = jnp.broadcast_to(meta_tokens.astype(x.dtype)[None], (B, N_META, D))
    h = jnp.concatenate([meta, x, jnp.zeros((B, L_pad - L, D), x.dtype)], axis=1)
    cos, sin = rope_tables(L_pad)
    for i in range(DEPTH):
        j = i // N_MIXERS
        a = rmsnorm(h, norm_mix_pre[i])
        if i % N_MIXERS == 0:
            m = mla_mixer(a, mla_w_dqkv[j], mla_q_norm[j], mla_w_uq[j], mla_kv_norm[j],
                          mla_w_ukv[j], mla_w_o[j], cos, sin)
        else:
            m = pool_mixer(a, pool_w_group[j], pool_scale[j])
        h = h + rmsnorm(m, norm_mix_post[i])
        f = conv_ffn(rmsnorm(h, norm_ffn_pre[i]), ffn_w_up[i], ffn_conv_w[i], ffn_conv_b[i], ffn_w_down[i])
        h = h + rmsnorm(f, norm_ffn_post[i])
    return h[:, N_META:L]
```

```python
import functools
import math

import jax
import jax.numpy as jnp
from jax import lax
from jax.experimental import pallas as pl
from jax.experimental.pallas import tpu as pltpu

N_META = 16
N_HEADS = 16
QK_NOPE = 64
QK_ROPE = 32
QK_HEAD = QK_NOPE + QK_ROPE
V_HEAD = 64
Q_RANK = 384
KV_RANK = 256
ROPE_THETA = 10000.0
Q_BLOCK = 128
POOL_WINDOWS = (2, 4, 8, 16)
CONV_WIDTH = 3
NORM_EPS = 1e-6

LANES = 128
SUBLANES = 8
SLOT = LANES
VMEM_LIMIT = 56 * 1024 * 1024

ROW_TILE = 544
ATT_TILE = 256
FF_CHUNK = 256
POOL_HALO = 16
NEG_BIG = -1e30

BF16 = jnp.bfloat16
F32 = jnp.float32


def _rmsnorm(x, g):
    ms = jnp.mean(x * x, axis=-1, keepdims=True)
    return x * lax.rsqrt(ms + NORM_EPS) * g


def _dot(a, b):
    return jnp.dot(a, b, preferred_element_type=F32)


def _const_spec(shape):
    nd = len(shape)
    return pl.BlockSpec(shape, lambda *_: (0,) * nd, pipeline_mode=pl.Buffered(1))


def _params():
    return pltpu.CompilerParams(
        dimension_semantics=("arbitrary", "arbitrary"),
        vmem_limit_bytes=VMEM_LIMIT,
    )


def _mla_front_kernel(h_ref, g_ref, wa_ref, qn_ref, kvn_ref, wqm_ref, wqr_ref,
                      wuk_ref, wuv_ref, cq_ref, sq_ref, ck_ref, sk_ref,
                      q_out, k_out, v_out):
    a = _rmsnorm(h_ref[0], g_ref[...]).astype(BF16)
    y = _dot(a, wa_ref[...])
    c_q = _rmsnorm(y[:, :Q_RANK], qn_ref[...]).astype(BF16)
    c_kv = _rmsnorm(y[:, Q_RANK:Q_RANK + KV_RANK], kvn_ref[...]).astype(BF16)
    o = Q_RANK + KV_RANK
    k_rope = y[:, o:o + SLOT] * ck_ref[...] + y[:, o + SLOT:o + 2 * SLOT] * sk_ref[...]
    cq2 = jnp.concatenate([cq_ref[...], cq_ref[...]], axis=1)
    sq2 = jnp.concatenate([sq_ref[...], sq_ref[...]], axis=1)
    kr2 = jnp.concatenate([k_rope, k_rope], axis=1)
    lane = lax.broadcasted_iota(jnp.int32, (1, 2 * SLOT), 1)
    ones_col = ((lane % SLOT) == V_HEAD).astype(F32)
    for hp in range(N_HEADS // 2):
        sl = slice(2 * SLOT * hp, 2 * SLOT * (hp + 1))
        q = _dot(c_q, wqm_ref[:, sl]) * cq2 + _dot(c_q, wqr_ref[:, sl]) * sq2
        k = _dot(c_kv, wuk_ref[:, sl]) + kr2
        v = _dot(c_kv, wuv_ref[:, sl]) + ones_col
        for e in range(2):
            q_out[0, 2 * hp + e] = q[:, e * SLOT:(e + 1) * SLOT].astype(BF16)
            k_out[0, 2 * hp + e] = k[:, e * SLOT:(e + 1) * SLOT].astype(BF16)
            v_out[0, 2 * hp + e] = v[:, e * SLOT:(e + 1) * SLOT].astype(BF16)


def _mla_front(h, g, wa, qn, kvn, wqm, wqr, wuk, wuv, cq, sq, ck, sk):
    B, LP, D = h.shape
    tm = ROW_TILE
    row = lambda b, t: (b, t, 0)
    tab = pl.BlockSpec((tm, SLOT), lambda b, t: (t, 0))
    head_out = pl.BlockSpec((1, N_HEADS, tm, SLOT), lambda b, t: (b, 0, t, 0))
    out_sds = jax.ShapeDtypeStruct((B, N_HEADS, LP, SLOT), BF16)
    return pl.pallas_call(
        _mla_front_kernel,
        grid=(B, LP // tm),
        in_specs=[
            pl.BlockSpec((1, tm, D), row),
            _const_spec(g.shape), _const_spec(wa.shape), _const_spec(qn.shape),
            _const_spec(kvn.shape), _const_spec(wqm.shape), _const_spec(wqr.shape),
            _const_spec(wuk.shape), _const_spec(wuv.shape),
            tab, tab, tab, tab,
        ],
        out_specs=[head_out, head_out, head_out],
        out_shape=[out_sds, out_sds, out_sds],
        compiler_params=_params(),
        name="mla_front",
    )(h, g, wa, qn, kvn, wqm, wqr, wuk, wuv, cq, sq, ck, sk)


def _attn_kernel(q_ref, k_ref, v_ref, h_ref, wo_ref, g_ref, out_ref,
                 m_scr, acc_scr, o_scr, *, tq, n_full, diag_start, diag_w):
    if n_full is None:
        n_full = pl.program_id(1)
        diag_start = pl.multiple_of(n_full * ATT_TILE, ATT_TILE)

    m_scr[...] = jnp.full(m_scr.shape, NEG_BIG, F32)
    acc_scr[...] = jnp.zeros(acc_scr.shape, F32)

    def update(kstart, width, mask):
        for hd in range(N_HEADS):
            kh = k_ref[0, hd, pl.ds(kstart, width), :]
            s = lax.dot_general(q_ref[0, hd], kh, (((1,), (1,)), ((), ())),
                                preferred_element_type=F32)
            if mask is not None:
                s = jnp.where(mask, s, NEG_BIG)
            m_old = m_scr[hd]
            m_new = jnp.maximum(m_old, jnp.max(s, axis=1, keepdims=True))
            alpha = jnp.exp2(m_old - m_new)
            m_rep = m_new if width == SLOT else jnp.concatenate([m_new] * (width // SLOT), axis=1)
            p = jnp.exp2(s - m_rep).astype(BF16)
            pv = _dot(p, v_ref[0, hd, pl.ds(kstart, width), :])
            acc_scr[hd] = alpha * acc_scr[hd] + pv
            m_scr[hd] = m_new

    def full_chunk(c, carry):
        update(pl.multiple_of(c * ATT_TILE, ATT_TILE), ATT_TILE, None)
        return carry

    lax.fori_loop(0, n_full, full_chunk, 0)
    rows = lax.broadcasted_iota(jnp.int32, (tq, diag_w), 0)
    cols = lax.broadcasted_iota(jnp.int32, (tq, diag_w), 1)
    update(diag_start, diag_w, cols <= rows)

    lane = lax.broadcasted_iota(jnp.int32, (tq, SLOT), 1)
    for hp in range(N_HEADS // 2):
        acc_e = acc_scr[2 * hp]
        acc_o = acc_scr[2 * hp + 1]
        o_e = acc_e * (1.0 / acc_e[:, V_HEAD:V_HEAD + 1])
        o_o = acc_o * (1.0 / acc_o[:, V_HEAD:V_HEAD + 1])
        pair = jnp.where(lane < V_HEAD, o_e, pltpu.roll(o_o, V_HEAD, axis=1))
        o_scr[:, SLOT * hp:SLOT * (hp + 1)] = pair.astype(BF16)
    mix = _dot(o_scr[...], wo_ref[...])
    out_ref[0] = h_ref[0] + _rmsnorm(mix, g_ref[...])


def _attention(h, q, k, v, wo, g, *, tq, grid_q, q_block0, n_full, diag_start, diag_w):
    B, LP, D = h.shape
    hrow = lambda b, j: (b, j + q_block0, 0)
    kern = functools.partial(_attn_kernel, tq=tq, n_full=n_full,
                             diag_start=diag_start, diag_w=diag_w)
    kv_spec = pl.BlockSpec((1, N_HEADS, LP, SLOT), lambda b, j: (b, 0, 0, 0))
    return pl.pallas_call(
        kern,
        grid=(B, grid_q),
        in_specs=[
            pl.BlockSpec((1, N_HEADS, tq, SLOT), lambda b, j: (b, 0, j + q_block0, 0)),
            kv_spec, kv_spec,
            pl.BlockSpec((1, tq, D), hrow),
            _const_spec(wo.shape), _const_spec(g.shape),
        ],
        out_specs=pl.BlockSpec((1, tq, D), hrow),
        out_shape=jax.ShapeDtypeStruct(h.shape, h.dtype),
        scratch_shapes=[
            pltpu.VMEM((N_HEADS, tq, SLOT), F32),
            pltpu.VMEM((N_HEADS, tq, SLOT), F32),
            pltpu.VMEM((tq, N_HEADS * V_HEAD), BF16),
        ],
        input_output_aliases={3: 0},
        compiler_params=_params(),
        name="mla_attn_t%d" % tq,
    )(q, k, v, h, wo, g)


def _pool_kernel(h_ref, gpre_ref, wg_ref, scale_ref, gpost_ref, out_ref, halo_scr, *, tm):
    t = pl.program_id(1)

    @pl.when(t == 0)
    def _():
        halo_scr[...] = jnp.zeros(halo_scr.shape, F32)

    x = h_ref[0]
    a = _rmsnorm(x, gpre_ref[...])
    ext = jnp.concatenate([halo_scr[...], a], axis=0)
    halo_scr[...] = a[tm - POOL_HALO:, :]
    gd = a.shape[1] // len(POOL_WINDOWS)
    pos1 = (lax.broadcasted_iota(jnp.int32, (tm, gd), 0) + t * tm + 1).astype(F32)
    ys = []
    for g, w in enumerate(POOL_WINDOWS):
        s = ext[:, g * gd:(g + 1) * gd]
        span = 1
        while span < w:
            s = s + pltpu.roll(s, span, axis=0)
            span *= 2
        inv_cnt = 1.0 / jnp.minimum(pos1, float(w))
        mixed = s[POOL_HALO:, :] * inv_cnt - a[:, g * gd:(g + 1) * gd]
        ys.append(_dot(mixed.astype(BF16), wg_ref[g]))
    y = jnp.concatenate(ys, axis=1) * scale_ref[...]
    out_ref[0] = x + _rmsnorm(y, gpost_ref[...])


def _pool_mixer(h, gpre, wg, scale, gpost):
    B, LP, D = h.shape
    tm = ROW_TILE
    row = lambda b, t: (b, t, 0)
    return pl.pallas_call(
        functools.partial(_pool_kernel, tm=tm),
        grid=(B, LP // tm),
        in_specs=[
            pl.BlockSpec((1, tm, D), row),
            _const_spec(gpre.shape), _const_spec(wg.shape),
            _const_spec(scale.shape), _const_spec(gpost.shape),
        ],
        out_specs=pl.BlockSpec((1, tm, D), row),
        out_shape=jax.ShapeDtypeStruct(h.shape, h.dtype),
        scratch_shapes=[pltpu.VMEM((POOL_HALO, D), F32)],
        compiler_params=_params(),
        name="pool_mixer",
    )(h, gpre, wg, scale, gpost)


def _ffn_kernel(h_ref, gpre_ref, wup_ref, cw_ref, cb_ref, wdn_ref, gpost_ref, out_ref,
                tail_scr, acc_scr, *, tm, d_ff):
    t = pl.program_id(1)

    @pl.when(t == 0)
    def _():
        tail_scr[...] = jnp.zeros(tail_scr.shape, F32)

    x = h_ref[0]
    hn = _rmsnorm(x, gpre_ref[...]).astype(BF16)

    def conv(col0):
        sl = slice(col0, col0 + FF_CHUNK)
        u = _dot(hn, wup_ref[:, sl])
        ext = jnp.concatenate([tail_scr[:, sl], u], axis=0)
        tail_scr[:, sl] = u[tm - SUBLANES:, :]
        y = (cb_ref[:, sl] + cw_ref[2:3, sl] * ext
             + cw_ref[1:2, sl] * pltpu.roll(ext, 1, axis=0)
             + cw_ref[0:1, sl] * pltpu.roll(ext, 2, axis=0))
        return y[SUBLANES:, :]

    for c in range(d_ff // FF_CHUNK):
        gate = conv(c * FF_CHUNK)
        val = conv(d_ff + c * FF_CHUNK)
        act = (gate * (1.0 / (1.0 + jnp.exp(-gate))) * val).astype(BF16)
        part = _dot(act, wdn_ref[c * FF_CHUNK:(c + 1) * FF_CHUNK, :])
        if c == 0:
            acc_scr[...] = part
        else:
            acc_scr[...] += part
    out_ref[0] = x + _rmsnorm(acc_scr[...], gpost_ref[...])


def _conv_ffn(h, gpre, wup, cw, cb, wdn, gpost):
    B, LP, D = h.shape
    tm = ROW_TILE
    d_ff = wdn.shape[0]
    row = lambda b, t: (b, t, 0)
    return pl.pallas_call(
        functools.partial(_ffn_kernel, tm=tm, d_ff=d_ff),
        grid=(B, LP // tm),
        in_specs=[
            pl.BlockSpec((1, tm, D), row),
            _const_spec(gpre.shape), _const_spec(wup.shape), _const_spec(cw.shape),
            _const_spec(cb.shape), _const_spec(wdn.shape), _const_spec(gpost.shape),
        ],
        out_specs=pl.BlockSpec((1, tm, D), row),
        out_shape=jax.ShapeDtypeStruct(h.shape, h.dtype),
        scratch_shapes=[
            pltpu.VMEM((SUBLANES, 2 * d_ff), F32),
            pltpu.VMEM((tm, D), F32),
        ],
        compiler_params=_params(),
        name="conv_ffn",
    )(h, gpre, wup, cw, cb, wdn, gpost)


def _rope_tables(lp):
    inv = 1.0 / (ROPE_THETA ** (jnp.arange(0, QK_ROPE, 2, dtype=F32) / QK_ROPE))
    ang = jnp.arange(lp, dtype=F32)[:, None] * inv[None, :]
    cos, sin = jnp.cos(ang), jnp.sin(ang)
    z_nope = jnp.zeros((lp, QK_NOPE), F32)
    z_pad = jnp.zeros((lp, SLOT - QK_HEAD), F32)
    c_k = jnp.concatenate([z_nope, cos, cos, z_pad], axis=1)
    s_k = jnp.concatenate([z_nope, sin, sin, z_pad], axis=1)
    qs = (QK_HEAD ** -0.5) * math.log2(math.e)
    c_q = jnp.concatenate([jnp.ones((lp, QK_NOPE), F32), cos, cos, z_pad], axis=1) * qs
    return c_q, s_k * qs, c_k, s_k


def _rot_half(w):
    half = QK_ROPE // 2
    return jnp.concatenate([-w[..., half:], w[..., :half]], axis=-1)


def _mla_weights(w_dqkv, w_uq, w_ukv):
    d = w_dqkv.shape[0]
    o = Q_RANK + KV_RANK
    wkr = w_dqkv[:, o:]
    z = lambda *s: jnp.zeros(s, F32)
    pad = SLOT - QK_HEAD
    wa = jnp.concatenate([
        w_dqkv[:, :o],
        z(d, QK_NOPE), wkr, z(d, pad),
        z(d, QK_NOPE), _rot_half(wkr), z(d, pad)], axis=1).astype(BF16)
    wq = w_uq.reshape(Q_RANK, N_HEADS, QK_HEAD)
    nope, rope = wq[..., :QK_NOPE], wq[..., QK_NOPE:]
    wqm = jnp.concatenate([nope, rope, z(Q_RANK, N_HEADS, pad)], axis=-1)
    wqr = jnp.concatenate([z(Q_RANK, N_HEADS, QK_NOPE), _rot_half(rope),
                           z(Q_RANK, N_HEADS, pad)], axis=-1)
    wkv = w_ukv.reshape(KV_RANK, N_HEADS, QK_NOPE + V_HEAD)
    wuk = jnp.concatenate([wkv[..., :QK_NOPE], z(KV_RANK, N_HEADS, SLOT - QK_NOPE)], axis=-1)
    wuv = jnp.concatenate([wkv[..., QK_NOPE:], z(KV_RANK, N_HEADS, SLOT - V_HEAD)], axis=-1)
    flat = lambda w: w.reshape(w.shape[0], N_HEADS * SLOT).astype(BF16)
    return wa, flat(wqm), flat(wqr), flat(wuk), flat(wuv)


def kernel(x, meta_tokens, norm_mix_pre, norm_mix_post, norm_ffn_pre, norm_ffn_post,
           mla_w_dqkv, mla_q_norm, mla_w_uq, mla_kv_norm, mla_w_ukv, mla_w_o,
           pool_w_group, pool_scale, ffn_w_up, ffn_conv_w, ffn_conv_b, ffn_w_down):
    B, S, D = x.shape
    depth = norm_mix_pre.shape[0]
    L = N_META + S
    LP = -(-L // Q_BLOCK) * Q_BLOCK
    assert LP % ROW_TILE == 0 and LP % ATT_TILE == Q_BLOCK
    meta = jnp.broadcast_to(meta_tokens.astype(x.dtype)[None], (B, N_META, D))
    h = jnp.concatenate([meta, x, jnp.zeros((B, LP - L, D), x.dtype)], axis=1)
    c_q, s_q, c_k, s_k = _rope_tables(LP)
    row = lambda v: v.reshape(1, -1)
    n_main = LP // ATT_TILE
    for i in range(depth):
        j = i // 2
        if i % 2 == 0:
            wa, wqm, wqr, wuk, wuv = _mla_weights(mla_w_dqkv[j], mla_w_uq[j], mla_w_ukv[j])
            q, k, v = _mla_front(h, row(norm_mix_pre[i]), wa, row(mla_q_norm[j]),
                                 row(mla_kv_norm[j]), wqm, wqr, wuk, wuv, c_q, s_q, c_k, s_k)
            wo = mla_w_o[j].astype(BF16)
            g = row(norm_mix_post[i])
            h = _attention(h, q, k, v, wo, g, tq=ATT_TILE, grid_q=n_main, q_block0=0,
                           n_full=None, diag_start=None, diag_w=ATT_TILE)
            h = _attention(h, q, k, v, wo, g, tq=Q_BLOCK, grid_q=1,
                           q_block0=n_main * ATT_TILE // Q_BLOCK, n_full=n_main,
                           diag_start=n_main * ATT_TILE, diag_w=Q_BLOCK)
        else:
            h = _pool_mixer(h, row(norm_mix_pre[i]), pool_w_group[j].astype(BF16),
                            row(pool_scale[j]), row(norm_mix_post[i]))
        h = _conv_ffn(h, row(norm_ffn_pre[i]), ffn_w_up[i].astype(BF16), ffn_conv_w[i],
                      row(ffn_conv_b[i]), ffn_w_down[i].astype(BF16), row(norm_ffn_post[i]))
    return h[:, N_META:L]
```

```python
import functools
import math

import jax
import jax.numpy as jnp
from jax import lax
from jax.experimental import pallas as pl
from jax.experimental.pallas import tpu as pltpu

N_META = 16
N_HEADS = 16
QK_NOPE = 64
QK_ROPE = 32
QK_HEAD = QK_NOPE + QK_ROPE
V_HEAD = 64
Q_RANK = 384
KV_RANK = 256
ROPE_THETA = 10000.0
POOL_WINDOWS = (2, 4, 8, 16)
NORM_EPS = 1e-6

LANES = 128
BF16_ROWS = 16
SLOT = LANES
V_ROWS = V_HEAD + BF16_ROWS
VMEM_LIMIT = 56 * 1024 * 1024

META_ROWS = 128
HIST = 16
ROW_TILE = 512
ATT_TILE = 256
FF_CHUNK = 256
AHEAD = 6
WIDE = 2
AHEAD_WIDE = 3
NEG_BIG = -1e30

BF16 = jnp.bfloat16
F32 = jnp.float32
NT = (((1,), (1,)), ((), ()))
TN = (((0,), (0,)), ((), ()))


def _rmsnorm(x, g):
    ms = jnp.mean(x * x, axis=-1, keepdims=True)
    return x * lax.rsqrt(ms + NORM_EPS) * g


def _dot(a, b):
    return jnp.dot(a, b, preferred_element_type=F32)


def _const_spec(shape):
    nd = len(shape)
    return pl.BlockSpec(shape, lambda *_: (0,) * nd, pipeline_mode=pl.Buffered(1))


def _params():
    return pltpu.CompilerParams(
        dimension_semantics=("arbitrary", "arbitrary"),
        vmem_limit_bytes=VMEM_LIMIT,
    )


def _history_specs(tm, d):
    per_tile = tm // HIST
    prev = pl.BlockSpec((1, HIST, d), lambda b, t: (b, jnp.maximum(t * per_tile - 1, 0), 0))
    meta = pl.BlockSpec((1, HIST, d), lambda b, t: (0, 0, 0))
    return prev, meta


def _with_history(x, prev_ref, meta_ref, has_hist):
    if has_hist:
        prev = jnp.where(pl.program_id(1) == 0, meta_ref[0], prev_ref[0])
    else:
        prev = jnp.zeros((HIST, x.shape[1]), F32)
    return jnp.concatenate([prev, x], axis=0)


def _mla_front_kernel(h_ref, g_ref, wa_ref, qn_ref, kvn_ref, wqm_ref, wqr_ref,
                      wuk_ref, wuvt_ref, cq_ref, sq_ref, ck_ref, sk_ref,
                      q_out, k_out, vt_out):
    tm = h_ref.shape[1]
    a = _rmsnorm(h_ref[0], g_ref[...]).astype(BF16)
    y = _dot(a, wa_ref[...])
    c_q = _rmsnorm(y[:, :Q_RANK], qn_ref[...]).astype(BF16)
    c_kv = _rmsnorm(y[:, Q_RANK:Q_RANK + KV_RANK], kvn_ref[...]).astype(BF16)
    o = Q_RANK + KV_RANK
    k_rope = y[:, o:o + SLOT] * ck_ref[...] + y[:, o + SLOT:o + 2 * SLOT] * sk_ref[...]
    cq2 = jnp.concatenate([cq_ref[...], cq_ref[...]], axis=1)
    sq2 = jnp.concatenate([sq_ref[...], sq_ref[...]], axis=1)
    kr2 = jnp.concatenate([k_rope, k_rope], axis=1)
    for hp in range(N_HEADS // 2):
        sl = slice(2 * SLOT * hp, 2 * SLOT * (hp + 1))
        q = _dot(c_q, wqm_ref[:, sl]) * cq2 + _dot(c_q, wqr_ref[:, sl]) * sq2
        k = _dot(c_kv, wuk_ref[:, sl]) + kr2
        for e in range(2):
            q_out[0, 2 * hp + e] = q[:, e * SLOT:(e + 1) * SLOT].astype(BF16)
            k_out[0, 2 * hp + e] = k[:, e * SLOT:(e + 1) * SLOT].astype(BF16)
    ones_row = (lax.broadcasted_iota(jnp.int32, (V_ROWS, tm), 0) == V_HEAD).astype(F32)
    for hd in range(N_HEADS):
        vt = lax.dot_general(wuvt_ref[hd], c_kv, NT, preferred_element_type=F32)
        vt_out[0, hd, 0] = (vt + ones_row).astype(BF16)


def _mla_front(h, tm, g, wa, qn, kvn, wqm, wqr, wuk, wuvt, tabs):
    B, L, D = h.shape
    tab = pl.BlockSpec((tm, SLOT), lambda b, t: (t, 0))
    head_out = pl.BlockSpec((1, N_HEADS, tm, SLOT), lambda b, t: (b, 0, t, 0))
    qk_sds = jax.ShapeDtypeStruct((B, N_HEADS, L, SLOT), BF16)
    vt_sds = jax.ShapeDtypeStruct((B, N_HEADS, L // tm, V_ROWS, tm), BF16)
    return pl.pallas_call(
        _mla_front_kernel,
        grid=(B, L // tm),
        in_specs=[
            pl.BlockSpec((1, tm, D), lambda b, t: (b, t, 0)),
            _const_spec(g.shape), _const_spec(wa.shape), _const_spec(qn.shape),
            _const_spec(kvn.shape), _const_spec(wqm.shape), _const_spec(wqr.shape),
            _const_spec(wuk.shape), _const_spec(wuvt.shape),
            tab, tab, tab, tab,
        ],
        out_specs=[head_out, head_out,
                   pl.BlockSpec((1, N_HEADS, 1, V_ROWS, tm), lambda b, t: (b, 0, t, 0, 0))],
        out_shape=[qk_sds, qk_sds, vt_sds],
        compiler_params=_params(),
        name="mla_front_%d" % tm,
    )(h, g, wa, qn, kvn, wqm, wqr, wuk, wuvt, *tabs)


def _attn_kernel(q_ref, k_ref, vt_ref, km_ref, vtm_ref, h_ref, wo_ref, g_ref, out_ref,
                 m_scr, acc_scr, ot_scr, *, tq, has_meta):
    j = pl.program_id(1)

    def scores(keys, hd):
        return lax.dot_general(keys, q_ref[0, hd], NT, preferred_element_type=F32)

    def first(hd, s_pair):
        s_diag, s_meta = s_pair
        m = jnp.max(s_diag, axis=0, keepdims=True)
        if has_meta:
            m = jnp.maximum(m, jnp.max(s_meta, axis=0, keepdims=True))
        acc = _dot(vt_ref[0, hd, j], jnp.exp2(s_diag - m).astype(BF16))
        if has_meta:
            p_meta = jnp.concatenate([jnp.exp2(s_meta - m).astype(BF16),
                                      jnp.zeros((META_ROWS - N_META, tq), BF16)], axis=0)
            acc = acc + _dot(vtm_ref[0, hd, 0], p_meta)
        acc_scr[hd] = acc
        m_scr[hd] = m

    def update(hd, s, vt):
        m_old = m_scr[hd]
        m_new = jnp.maximum(m_old, jnp.max(s, axis=0, keepdims=True))
        alpha = jnp.exp2(m_old - m_new)
        p = jnp.exp2(s - m_new).astype(BF16)
        acc_scr[hd] = alpha * acc_scr[hd] + _dot(vt, p)
        m_scr[hd] = m_new

    key_i = lax.broadcasted_iota(jnp.int32, (tq, tq), 0)
    qry_i = lax.broadcasted_iota(jnp.int32, (tq, tq), 1)
    causal = key_i <= qry_i
    diag0 = pl.multiple_of(j * tq, tq)

    def first_scores(hd):
        s_diag = jnp.where(causal, scores(k_ref[0, hd, pl.ds(diag0, tq), :], hd), NEG_BIG)
        s_meta = scores(km_ref[0, hd, :N_META, :], hd) if has_meta else None
        return s_diag, s_meta

    def all_heads(score_fn, step_fn, ahead):
        pending = [score_fn(hd) for hd in range(ahead)]
        for hd in range(N_HEADS):
            if hd + ahead < N_HEADS:
                pending.append(score_fn(hd + ahead))
            step_fn(hd, pending.pop(0))

    all_heads(first_scores, first, AHEAD)

    if has_meta:
        def chunks(c, n, ahead):
            k0 = pl.multiple_of(c * tq, tq)

            def vt_of(hd):
                parts = [vt_ref[0, hd, c + i] for i in range(n)]
                return parts[0] if n == 1 else jnp.concatenate(parts, axis=1)

            all_heads(lambda hd: scores(k_ref[0, hd, pl.ds(k0, n * tq), :], hd),
                      lambda hd, s: update(hd, s, vt_of(hd)), ahead)

        def wide_step(i, carry):
            chunks(WIDE * i, WIDE, AHEAD_WIDE)
            return carry

        lax.fori_loop(0, j // WIDE, wide_step, 0)

        @pl.when(j % WIDE == 1)
        def _():
            chunks(j - 1, 1, AHEAD)

    for hd in range(N_HEADS):
        acc = acc_scr[hd]
        o = acc[:V_HEAD] * (1.0 / acc[V_HEAD:V_HEAD + 1])
        ot_scr[V_HEAD * hd:V_HEAD * (hd + 1), :] = o.astype(BF16)
    mix = lax.dot_general(ot_scr[...], wo_ref[...], TN, preferred_element_type=F32)
    out_ref[0] = h_ref[0] + _rmsnorm(mix, g_ref[...])


def _attention(h, q, k, vt, k_meta, vt_meta, wo, g, *, tq, has_meta):
    B, L, D = h.shape
    row = lambda b, j: (b, j, 0)
    whole = lambda a: pl.BlockSpec((1,) + a.shape[1:], lambda b, j: (b,) + (0,) * (a.ndim - 1))
    shared = lambda a: pl.BlockSpec((1,) + a.shape[1:], lambda b, j: (0,) * a.ndim)
    return pl.pallas_call(
        functools.partial(_attn_kernel, tq=tq, has_meta=has_meta),
        grid=(B, L // tq),
        in_specs=[
            pl.BlockSpec((1, N_HEADS, tq, SLOT), lambda b, j: (b, 0, j, 0)),
            whole(k), whole(vt), shared(k_meta), shared(vt_meta),
            pl.BlockSpec((1, tq, D), row),
            _const_spec(wo.shape), _const_spec(g.shape),
        ],
        out_specs=pl.BlockSpec((1, tq, D), row),
        out_shape=jax.ShapeDtypeStruct(h.shape, h.dtype),
        scratch_shapes=[
            pltpu.VMEM((N_HEADS, 1, tq), F32),
            pltpu.VMEM((N_HEADS, V_ROWS, tq), F32),
            pltpu.VMEM((N_HEADS * V_HEAD, tq), BF16),
        ],
        compiler_params=_params(),
        name="mla_attn_%d" % tq,
    )(q, k, vt, k_meta, vt_meta, h, wo, g)


def _pool_kernel(h_ref, prev_ref, meta_ref, gpre_ref, wg_ref, scale_ref, gpost_ref, out_ref,
                 *, pos0, has_hist):
    tm = h_ref.shape[1]
    x = h_ref[0]
    a = _rmsnorm(_with_history(x, prev_ref, meta_ref, has_hist), gpre_ref[...])
    gd = x.shape[1] // len(POOL_WINDOWS)
    pos1 = lax.broadcasted_iota(jnp.int32, (tm, gd), 0) + (pl.program_id(1) * tm + pos0 + 1)
    pos1 = pos1.astype(F32)
    ys = []
    for g, w in enumerate(POOL_WINDOWS):
        a_g = a[:, g * gd:(g + 1) * gd]
        s = a_g
        span = 1
        while span < w:
            s = s + pltpu.roll(s, span, axis=0)
            span *= 2
        inv_cnt = 1.0 / jnp.minimum(pos1, float(w))
        mixed = s[HIST:, :] * inv_cnt - a_g[HIST:, :]
        ys.append(_dot(mixed.astype(BF16), wg_ref[g]))
    y = jnp.concatenate(ys, axis=1) * scale_ref[...]
    out_ref[0] = x + _rmsnorm(y, gpost_ref[...])


def _pool_mixer(h, meta_h, tm, pos0, has_hist, gpre, wg, scale, gpost):
    B, L, D = h.shape
    row = lambda b, t: (b, t, 0)
    prev, meta = _history_specs(tm, D)
    return pl.pallas_call(
        functools.partial(_pool_kernel, pos0=pos0, has_hist=has_hist),
        grid=(B, L // tm),
        in_specs=[
            pl.BlockSpec((1, tm, D), row), prev, meta,
            _const_spec(gpre.shape), _const_spec(wg.shape),
            _const_spec(scale.shape), _const_spec(gpost.shape),
        ],
        out_specs=pl.BlockSpec((1, tm, D), row),
        out_shape=jax.ShapeDtypeStruct(h.shape, h.dtype),
        compiler_params=_params(),
        name="pool_mixer_%d" % tm,
    )(h, h, meta_h, gpre, wg, scale, gpost)


def _ffn_kernel(h_ref, prev_ref, meta_ref, gpre_ref, wup_ref, cw_ref, cb_ref, wdn_ref,
                gpost_ref, out_ref, act_scr, *, has_hist):
    d_ff = wdn_ref.shape[0]
    x = h_ref[0]
    hn = _rmsnorm(_with_history(x, prev_ref, meta_ref, has_hist), gpre_ref[...]).astype(BF16)

    def conv(col0):
        sl = slice(col0, col0 + FF_CHUNK)
        u = _dot(hn, wup_ref[:, sl])
        y = (cb_ref[:, sl] + cw_ref[2:3, sl] * u
             + cw_ref[1:2, sl] * pltpu.roll(u, 1, axis=0)
             + cw_ref[0:1, sl] * pltpu.roll(u, 2, axis=0))
        return y[HIST:, :]

    for c in range(d_ff // FF_CHUNK):
        gate = conv(c * FF_CHUNK)
        val = conv(d_ff + c * FF_CHUNK)
        act = gate * (1.0 / (1.0 + jnp.exp(-gate))) * val
        act_scr[:, c * FF_CHUNK:(c + 1) * FF_CHUNK] = act.astype(BF16)
    f = _dot(act_scr[...], wdn_ref[...])
    out_ref[0] = x + _rmsnorm(f, gpost_ref[...])


def _conv_ffn(h, meta_h, tm, has_hist, gpre, wup, cw, cb, wdn, gpost):
    B, L, D = h.shape
    d_ff = wdn.shape[0]
    assert d_ff % FF_CHUNK == 0
    row = lambda b, t: (b, t, 0)
    prev, meta = _history_specs(tm, D)
    return pl.pallas_call(
        functools.partial(_ffn_kernel, has_hist=has_hist),
        grid=(B, L // tm),
        in_specs=[
            pl.BlockSpec((1, tm, D), row), prev, meta,
            _const_spec(gpre.shape), _const_spec(wup.shape), _const_spec(cw.shape),
            _const_spec(cb.shape), _const_spec(wdn.shape), _const_spec(gpost.shape),
        ],
        out_specs=pl.BlockSpec((1, tm, D), row),
        out_shape=jax.ShapeDtypeStruct(h.shape, h.dtype),
        scratch_shapes=[pltpu.VMEM((tm, d_ff), BF16)],
        compiler_params=_params(),
        name="conv_ffn_%d" % tm,
    )(h, h, meta_h, gpre, wup, cw, cb, wdn, gpost)


def _rope_tables(pos):
    n = pos.shape[0]
    inv = 1.0 / (ROPE_THETA ** (jnp.arange(0, QK_ROPE, 2, dtype=F32) / QK_ROPE))
    ang = pos.astype(F32)[:, None] * inv[None, :]
    cos, sin = jnp.cos(ang), jnp.sin(ang)
    z_nope = jnp.zeros((n, QK_NOPE), F32)
    z_pad = jnp.zeros((n, SLOT - QK_HEAD), F32)
    c_k = jnp.concatenate([z_nope, cos, cos, z_pad], axis=1)
    s_k = jnp.concatenate([z_nope, sin, sin, z_pad], axis=1)
    qs = (QK_HEAD ** -0.5) * math.log2(math.e)
    c_q = jnp.concatenate([jnp.ones((n, QK_NOPE), F32), cos, cos, z_pad], axis=1) * qs
    return c_q, s_k * qs, c_k, s_k


def _rot_half(w):
    half = QK_ROPE // 2
    return jnp.concatenate([-w[..., half:], w[..., :half]], axis=-1)


def _mla_weights(w_dqkv, w_uq, w_ukv):
    d = w_dqkv.shape[0]
    o = Q_RANK + KV_RANK
    wkr = w_dqkv[:, o:]
    z = lambda *s: jnp.zeros(s, F32)
    pad = SLOT - QK_HEAD
    wa = jnp.concatenate([
        w_dqkv[:, :o],
        z(d, QK_NOPE), wkr, z(d, pad),
        z(d, QK_NOPE), _rot_half(wkr), z(d, pad)], axis=1).astype(BF16)
    wq = w_uq.reshape(Q_RANK, N_HEADS, QK_HEAD)
    nope, rope = wq[..., :QK_NOPE], wq[..., QK_NOPE:]
    wqm = jnp.concatenate([nope, rope, z(Q_RANK, N_HEADS, pad)], axis=-1)
    wqr = jnp.concatenate([z(Q_RANK, N_HEADS, QK_NOPE), _rot_half(rope),
                           z(Q_RANK, N_HEADS, pad)], axis=-1)
    wkv = w_ukv.reshape(KV_RANK, N_HEADS, QK_NOPE + V_HEAD)
    wuk = jnp.concatenate([wkv[..., :QK_NOPE], z(KV_RANK, N_HEADS, SLOT - QK_NOPE)], axis=-1)
    wuvt = jnp.concatenate([jnp.transpose(wkv[..., QK_NOPE:], (1, 2, 0)),
                            z(N_HEADS, V_ROWS - V_HEAD, KV_RANK)], axis=1).astype(BF16)
    flat = lambda w: w.reshape(w.shape[0], N_HEADS * SLOT).astype(BF16)
    return wa, flat(wqm), flat(wqr), flat(wuk), wuvt


def kernel(x, meta_tokens, norm_mix_pre, norm_mix_post, norm_ffn_pre, norm_ffn_post,
           mla_w_dqkv, mla_q_norm, mla_w_uq, mla_kv_norm, mla_w_ukv, mla_w_o,
           pool_w_group, pool_scale, ffn_w_up, ffn_conv_w, ffn_conv_b, ffn_w_down):
    B, S, D = x.shape
    depth = norm_mix_pre.shape[0]
    assert meta_tokens.shape[0] == N_META == HIST and S % ROW_TILE == 0
    h = x
    hm = jnp.concatenate([meta_tokens.astype(x.dtype),
                          jnp.zeros((META_ROWS - N_META, D), x.dtype)], axis=0)[None]
    tabs_m = _rope_tables(jnp.arange(META_ROWS))
    tabs_t = _rope_tables(N_META + jnp.arange(S))
    row = lambda v: v.reshape(1, -1)
    for i in range(depth):
        j = i // 2
        g_pre, g_post = row(norm_mix_pre[i]), row(norm_mix_post[i])
        if i % 2 == 0:
            w = _mla_weights(mla_w_dqkv[j], mla_w_uq[j], mla_w_ukv[j])
            qn, kvn = row(mla_q_norm[j]), row(mla_kv_norm[j])
            wo = mla_w_o[j].astype(BF16)
            qm, km, vtm = _mla_front(hm, META_ROWS, g_pre, w[0], qn, kvn, *w[1:], tabs_m)
            q, k, vt = _mla_front(h, ATT_TILE, g_pre, w[0], qn, kvn, *w[1:], tabs_t)
            hm = _attention(hm, qm, km, vtm, km, vtm, wo, g_post, tq=META_ROWS, has_meta=False)
            h = _attention(h, q, k, vt, km, vtm, wo, g_post, tq=ATT_TILE, has_meta=True)
        else:
            wg = pool_w_group[j].astype(BF16)
            sc = row(pool_scale[j])
            hm_new = _pool_mixer(hm, hm, META_ROWS, 0, False, g_pre, wg, sc, g_post)
            h = _pool_mixer(h, hm, ROW_TILE, N_META, True, g_pre, wg, sc, g_post)
            hm = hm_new
        ffn_w = (row(norm_ffn_pre[i]), ffn_w_up[i].astype(BF16), ffn_conv_w[i],
                 row(ffn_conv_b[i]), ffn_w_down[i].astype(BF16), row(norm_ffn_post[i]))
        hm_new = _conv_ffn(hm, hm, META_ROWS, False, *ffn_w)
        h = _conv_ffn(h, hm, ROW_TILE, True, *ffn_w)
        hm = hm_new
    return h
```

```python
import functools
import math

import jax
import jax.numpy as jnp
from jax import lax
from jax.experimental import pallas as pl
from jax.experimental.pallas import tpu as pltpu

N_META = 16
N_HEADS = 16
QK_NOPE = 64
QK_ROPE = 32
QK_HEAD = QK_NOPE + QK_ROPE
V_HEAD = 64
Q_RANK = 384
KV_RANK = 256
ROPE_THETA = 10000.0
POOL_WINDOWS = (2, 4, 8, 16)
NORM_EPS = 1e-6

LANES = 128
BF16_ROWS = 16
SLOT = LANES
V_ROWS = V_HEAD + BF16_ROWS
VMEM_LIMIT = 56 * 1024 * 1024

META_ROWS = 128
HIST = 16
ROW_TILE = 512
FFN_TILE = 1024
ATT_TILE = 512
FF_CHUNK = 256
AHEAD = 4
AHEAD_FULL = 2
NEG_BIG = -1e30

BF16 = jnp.bfloat16
F32 = jnp.float32
NT = (((1,), (1,)), ((), ()))
TN = (((0,), (0,)), ((), ()))


def _rmsnorm(x, g):
    ms = jnp.mean(x * x, axis=-1, keepdims=True)
    return x * lax.rsqrt(ms + NORM_EPS) * g


def _dot(a, b):
    return jnp.dot(a, b, preferred_element_type=F32)


def _const_spec(shape):
    nd = len(shape)
    return pl.BlockSpec(shape, lambda *_: (0,) * nd, pipeline_mode=pl.Buffered(1))


def _params():
    return pltpu.CompilerParams(
        dimension_semantics=("arbitrary", "arbitrary"),
        vmem_limit_bytes=VMEM_LIMIT,
    )


def _history_specs(tm, d):
    per_tile = tm // HIST
    prev = pl.BlockSpec((1, HIST, d), lambda b, t: (b, jnp.maximum(t * per_tile - 1, 0), 0))
    meta = pl.BlockSpec((1, HIST, d), lambda b, t: (0, 0, 0))
    return prev, meta


def _with_history(x, prev_ref, meta_ref, has_hist):
    if has_hist:
        prev = jnp.where(pl.program_id(1) == 0, meta_ref[0], prev_ref[0])
    else:
        prev = jnp.zeros((HIST, x.shape[1]), F32)
    return jnp.concatenate([prev, x], axis=0)


def _mla_front_kernel(h_ref, g_ref, wa_ref, qn_ref, kvn_ref, wqm_ref, wqr_ref,
                      wuk_ref, wuvt_ref, cq_ref, sq_ref, ck_ref, sk_ref,
                      q_out, k_out, vt_out):
    tm = h_ref.shape[1]
    a = _rmsnorm(h_ref[0], g_ref[...]).astype(BF16)
    y = _dot(a, wa_ref[...])
    c_q = _rmsnorm(y[:, :Q_RANK], qn_ref[...]).astype(BF16)
    c_kv = _rmsnorm(y[:, Q_RANK:Q_RANK + KV_RANK], kvn_ref[...]).astype(BF16)
    o = Q_RANK + KV_RANK
    k_rope = y[:, o:o + SLOT] * ck_ref[...] + y[:, o + SLOT:o + 2 * SLOT] * sk_ref[...]
    cq2 = jnp.concatenate([cq_ref[...], cq_ref[...]], axis=1)
    sq2 = jnp.concatenate([sq_ref[...], sq_ref[...]], axis=1)
    kr2 = jnp.concatenate([k_rope, k_rope], axis=1)
    rot_all = _dot(c_q, wqr_ref[...])
    heads_per_blk = SLOT // QK_ROPE

    def rot_slot(hd):
        blk = rot_all[:, SLOT * (hd // heads_per_blk):SLOT * (hd // heads_per_blk + 1)]
        shift = (QK_NOPE - QK_ROPE * (hd % heads_per_blk)) % SLOT
        return blk if shift == 0 else pltpu.roll(blk, shift, axis=1)

    for hp in range(N_HEADS // 2):
        sl = slice(2 * SLOT * hp, 2 * SLOT * (hp + 1))
        rot = jnp.concatenate([rot_slot(2 * hp), rot_slot(2 * hp + 1)], axis=1)
        q = _dot(c_q, wqm_ref[:, sl]) * cq2 + rot * sq2
        k = _dot(c_kv, wuk_ref[:, sl]) + kr2
        for e in range(2):
            q_out[0, 2 * hp + e] = q[:, e * SLOT:(e + 1) * SLOT].astype(BF16)
            k_out[0, 2 * hp + e] = k[:, e * SLOT:(e + 1) * SLOT].astype(BF16)
    ones_row = (lax.broadcasted_iota(jnp.int32, (V_ROWS, tm), 0) == V_HEAD).astype(F32)
    for hd in range(N_HEADS):
        vt = lax.dot_general(wuvt_ref[hd], c_kv, NT, preferred_element_type=F32)
        vt_out[0, hd, 0] = (vt + ones_row).astype(BF16)


def _mla_front(h, tm, g, wa, qn, kvn, wqm, wqr, wuk, wuvt, tabs):
    B, L, D = h.shape
    tab = pl.BlockSpec((tm, SLOT), lambda b, t: (t, 0))
    head_out = pl.BlockSpec((1, N_HEADS, tm, SLOT), lambda b, t: (b, 0, t, 0))
    qk_sds = jax.ShapeDtypeStruct((B, N_HEADS, L, SLOT), BF16)
    vt_sds = jax.ShapeDtypeStruct((B, N_HEADS, L // tm, V_ROWS, tm), BF16)
    return pl.pallas_call(
        _mla_front_kernel,
        grid=(B, L // tm),
        in_specs=[
            pl.BlockSpec((1, tm, D), lambda b, t: (b, t, 0)),
            _const_spec(g.shape), _const_spec(wa.shape), _const_spec(qn.shape),
            _const_spec(kvn.shape), _const_spec(wqm.shape), _const_spec(wqr.shape),
            _const_spec(wuk.shape), _const_spec(wuvt.shape),
            tab, tab, tab, tab,
        ],
        out_specs=[head_out, head_out,
                   pl.BlockSpec((1, N_HEADS, 1, V_ROWS, tm), lambda b, t: (b, 0, t, 0, 0))],
        out_shape=[qk_sds, qk_sds, vt_sds],
        compiler_params=_params(),
        name="mla_front_%d" % tm,
    )(h, g, wa, qn, kvn, wqm, wqr, wuk, wuvt, *tabs)


def _attn_kernel(q_ref, k_ref, vt_ref, km_ref, vtm_ref, h_ref, wo_ref, g_ref, out_ref,
                 m_scr, acc_scr, ot_scr, *, tq, has_meta, split_diag):
    j = pl.program_id(1)

    def scores(keys, hd):
        return lax.dot_general(keys, q_ref[0, hd], NT, preferred_element_type=F32)

    kd = tq // 2 if split_diag else tq
    diag0 = pl.multiple_of(j * tq, tq)
    causal_a = (lax.broadcasted_iota(jnp.int32, (kd, tq), 0)
                <= lax.broadcasted_iota(jnp.int32, (kd, tq), 1))
    causal_b = causal_a[:, :kd]

    def first_scores(hd):
        s_a = jnp.where(causal_a, scores(k_ref[0, hd, pl.ds(diag0, kd), :], hd), NEG_BIG)
        s_b = s_meta = None
        if split_diag:
            s_b = lax.dot_general(k_ref[0, hd, pl.ds(diag0 + kd, kd), :], q_ref[0, hd, kd:, :],
                                  NT, preferred_element_type=F32)
            s_b = jnp.where(causal_b, s_b, NEG_BIG)
        if has_meta:
            s_meta = scores(km_ref[0, hd, :N_META, :], hd)
        return s_a, s_b, s_meta

    def first(hd, s_all):
        s_a, s_b, s_meta = s_all
        m = jnp.max(s_a, axis=0, keepdims=True)
        if has_meta:
            m = jnp.maximum(m, jnp.max(s_meta, axis=0, keepdims=True))
        if split_diag:
            m_hi = jnp.maximum(m[:, kd:], jnp.max(s_b, axis=0, keepdims=True))
            m = jnp.concatenate([m[:, :kd], m_hi], axis=1)
        vt = vt_ref[0, hd, j]
        acc = _dot(vt[:, :kd], jnp.exp2(s_a - m).astype(BF16))
        if has_meta:
            p_meta = jnp.concatenate([jnp.exp2(s_meta - m).astype(BF16),
                                      jnp.zeros((META_ROWS - N_META, tq), BF16)], axis=0)
            acc = acc + _dot(vtm_ref[0, hd, 0], p_meta)
        if split_diag:
            acc_hi = acc[:, kd:] + _dot(vt[:, kd:], jnp.exp2(s_b - m_hi).astype(BF16))
            acc = jnp.concatenate([acc[:, :kd], acc_hi], axis=1)
        acc_scr[hd] = acc
        m_scr[hd] = m

    def update(hd, s, vt):
        m_old = m_scr[hd]
        m_new = jnp.maximum(m_old, jnp.max(s, axis=0, keepdims=True))
        alpha = jnp.exp2(m_old - m_new)
        p = jnp.exp2(s - m_new).astype(BF16)
        acc_scr[hd] = alpha * acc_scr[hd] + _dot(vt, p)
        m_scr[hd] = m_new

    def all_heads(score_fn, step_fn, ahead):
        pending = [score_fn(hd) for hd in range(ahead)]
        for hd in range(N_HEADS):
            if hd + ahead < N_HEADS:
                pending.append(score_fn(hd + ahead))
            step_fn(hd, pending.pop(0))

    all_heads(first_scores, first, AHEAD)

    def full_chunk(c, carry):
        k0 = pl.multiple_of(c * tq, tq)
        all_heads(lambda hd: scores(k_ref[0, hd, pl.ds(k0, tq), :], hd),
                  lambda hd, s: update(hd, s, vt_ref[0, hd, c]), AHEAD_FULL)
        return carry

    lax.fori_loop(0, j, full_chunk, 0)

    for hd in range(N_HEADS):
        acc = acc_scr[hd]
        o = acc[:V_HEAD] * (1.0 / acc[V_HEAD:V_HEAD + 1])
        ot_scr[V_HEAD * hd:V_HEAD * (hd + 1), :] = o.astype(BF16)
    mix = lax.dot_general(ot_scr[...], wo_ref[...], TN, preferred_element_type=F32)
    out_ref[0] = h_ref[0] + _rmsnorm(mix, g_ref[...])


def _attention(h, q, k, vt, k_meta, vt_meta, wo, g, *, tq, has_meta, split_diag):
    B, L, D = h.shape
    row = lambda b, j: (b, j, 0)
    whole = lambda a: pl.BlockSpec((1,) + a.shape[1:], lambda b, j: (b,) + (0,) * (a.ndim - 1))
    shared = lambda a: pl.BlockSpec((1,) + a.shape[1:], lambda b, j: (0,) * a.ndim)
    return pl.pallas_call(
        functools.partial(_attn_kernel, tq=tq, has_meta=has_meta, split_diag=split_diag),
        grid=(B, L // tq),
        in_specs=[
            pl.BlockSpec((1, N_HEADS, tq, SLOT), lambda b, j: (b, 0, j, 0)),
            whole(k), whole(vt), shared(k_meta), shared(vt_meta),
            pl.BlockSpec((1, tq, D), row),
            _const_spec(wo.shape), _const_spec(g.shape),
        ],
        out_specs=pl.BlockSpec((1, tq, D), row),
        out_shape=jax.ShapeDtypeStruct(h.shape, h.dtype),
        scratch_shapes=[
            pltpu.VMEM((N_HEADS, 1, tq), F32),
            pltpu.VMEM((N_HEADS, V_ROWS, tq), F32),
            pltpu.VMEM((N_HEADS * V_HEAD, tq), BF16),
        ],
        compiler_params=_params(),
        name="mla_attn_%d" % tq,
    )(q, k, vt, k_meta, vt_meta, h, wo, g)


def _pool_kernel(h_ref, prev_ref, meta_ref, gpre_ref, wg_ref, scale_ref, gpost_ref, out_ref,
                 *, pos0, has_hist):
    tm = h_ref.shape[1]
    x = h_ref[0]
    a = _rmsnorm(_with_history(x, prev_ref, meta_ref, has_hist), gpre_ref[...])
    gd = x.shape[1] // len(POOL_WINDOWS)
    pos1 = lax.broadcasted_iota(jnp.int32, (tm, gd), 0) + (pl.program_id(1) * tm + pos0 + 1)
    pos1 = pos1.astype(F32)
    ys = []
    for g, w in enumerate(POOL_WINDOWS):
        a_g = a[:, g * gd:(g + 1) * gd]
        s = a_g
        span = 1
        while span < w:
            s = s + pltpu.roll(s, span, axis=0)
            span *= 2
        inv_cnt = 1.0 / jnp.minimum(pos1, float(w))
        mixed = s[HIST:, :] * inv_cnt - a_g[HIST:, :]
        ys.append(_dot(mixed.astype(BF16), wg_ref[g]))
    y = jnp.concatenate(ys, axis=1) * scale_ref[...]
    out_ref[0] = x + _rmsnorm(y, gpost_ref[...])


def _pool_mixer(h, meta_h, tm, pos0, has_hist, gpre, wg, scale, gpost):
    B, L, D = h.shape
    row = lambda b, t: (b, t, 0)
    prev, meta = _history_specs(tm, D)
    return pl.pallas_call(
        functools.partial(_pool_kernel, pos0=pos0, has_hist=has_hist),
        grid=(B, L // tm),
        in_specs=[
            pl.BlockSpec((1, tm, D), row), prev, meta,
            _const_spec(gpre.shape), _const_spec(wg.shape),
            _const_spec(scale.shape), _const_spec(gpost.shape),
        ],
        out_specs=pl.BlockSpec((1, tm, D), row),
        out_shape=jax.ShapeDtypeStruct(h.shape, h.dtype),
        compiler_params=_params(),
        name="pool_mixer_%d" % tm,
    )(h, h, meta_h, gpre, wg, scale, gpost)


def _ffn_kernel(h_ref, prev_ref, meta_ref, gpre_ref, wup_ref, cw_ref, cb_ref, wdn_ref,
                gpost_ref, out_ref, act_scr, *, has_hist):
    d_ff = wdn_ref.shape[0]
    x = h_ref[0]
    hn = _rmsnorm(_with_history(x, prev_ref, meta_ref, has_hist), gpre_ref[...]).astype(BF16)

    def conv(col0):
        sl = slice(col0, col0 + FF_CHUNK)
        u = _dot(hn, wup_ref[:, sl])
        y = (cb_ref[:, sl] + cw_ref[2:3, sl] * u
             + cw_ref[1:2, sl] * pltpu.roll(u, 1, axis=0)
             + cw_ref[0:1, sl] * pltpu.roll(u, 2, axis=0))
        return y[HIST:, :]

    for c in range(d_ff // FF_CHUNK):
        gate = conv(c * FF_CHUNK)
        val = conv(d_ff + c * FF_CHUNK)
        act = gate * (1.0 / (1.0 + jnp.exp(-gate))) * val
        act_scr[:, c * FF_CHUNK:(c + 1) * FF_CHUNK] = act.astype(BF16)
    f = _dot(act_scr[...], wdn_ref[...])
    out_ref[0] = x + _rmsnorm(f, gpost_ref[...])


def _conv_ffn(h, meta_h, tm, has_hist, gpre, wup, cw, cb, wdn, gpost):
    B, L, D = h.shape
    d_ff = wdn.shape[0]
    assert d_ff % FF_CHUNK == 0
    row = lambda b, t: (b, t, 0)
    prev, meta = _history_specs(tm, D)
    return pl.pallas_call(
        functools.partial(_ffn_kernel, has_hist=has_hist),
        grid=(B, L // tm),
        in_specs=[
            pl.BlockSpec((1, tm, D), row), prev, meta,
            _const_spec(gpre.shape), _const_spec(wup.shape), _const_spec(cw.shape),
            _const_spec(cb.shape), _const_spec(wdn.shape), _const_spec(gpost.shape),
        ],
        out_specs=pl.BlockSpec((1, tm, D), row),
        out_shape=jax.ShapeDtypeStruct(h.shape, h.dtype),
        scratch_shapes=[pltpu.VMEM((tm, d_ff), BF16)],
        compiler_params=_params(),
        name="conv_ffn_%d" % tm,
    )(h, h, meta_h, gpre, wup, cw, cb, wdn, gpost)


def _rope_tables(pos):
    n = pos.shape[0]
    inv = 1.0 / (ROPE_THETA ** (jnp.arange(0, QK_ROPE, 2, dtype=F32) / QK_ROPE))
    ang = pos.astype(F32)[:, None] * inv[None, :]
    cos, sin = jnp.cos(ang), jnp.sin(ang)
    z_nope = jnp.zeros((n, QK_NOPE), F32)
    z_pad = jnp.zeros((n, SLOT - QK_HEAD), F32)
    c_k = jnp.concatenate([z_nope, cos, cos, z_pad], axis=1)
    s_k = jnp.concatenate([z_nope, sin, sin, z_pad], axis=1)
    qs = (QK_HEAD ** -0.5) * math.log2(math.e)
    c_q = jnp.concatenate([jnp.ones((n, QK_NOPE), F32), cos, cos, z_pad], axis=1) * qs
    return c_q, s_k * qs, c_k, s_k


def _rot_half(w):
    half = QK_ROPE // 2
    return jnp.concatenate([-w[..., half:], w[..., :half]], axis=-1)


def _mla_weights(w_dqkv, w_uq, w_ukv):
    d = w_dqkv.shape[0]
    o = Q_RANK + KV_RANK
    wkr = w_dqkv[:, o:]
    z = lambda *s: jnp.zeros(s, F32)
    pad = SLOT - QK_HEAD
    wa = jnp.concatenate([
        w_dqkv[:, :o],
        z(d, QK_NOPE), wkr, z(d, pad),
        z(d, QK_NOPE), _rot_half(wkr), z(d, pad)], axis=1).astype(BF16)
    wq = w_uq.reshape(Q_RANK, N_HEADS, QK_HEAD)
    nope, rope = wq[..., :QK_NOPE], wq[..., QK_NOPE:]
    wqm = jnp.concatenate([nope, rope, z(Q_RANK, N_HEADS, pad)], axis=-1)
    wqr = _rot_half(rope).reshape(Q_RANK, N_HEADS * QK_ROPE).astype(BF16)
    wkv = w_ukv.reshape(KV_RANK, N_HEADS, QK_NOPE + V_HEAD)
    wuk = jnp.concatenate([wkv[..., :QK_NOPE], z(KV_RANK, N_HEADS, SLOT - QK_NOPE)], axis=-1)
    wuvt = jnp.concatenate([jnp.transpose(wkv[..., QK_NOPE:], (1, 2, 0)),
                            z(N_HEADS, V_ROWS - V_HEAD, KV_RANK)], axis=1).astype(BF16)
    flat = lambda w: w.reshape(w.shape[0], N_HEADS * SLOT).astype(BF16)
    return wa, flat(wqm), wqr, flat(wuk), wuvt


def kernel(x, meta_tokens, norm_mix_pre, norm_mix_post, norm_ffn_pre, norm_ffn_post,
           mla_w_dqkv, mla_q_norm, mla_w_uq, mla_kv_norm, mla_w_ukv, mla_w_o,
           pool_w_group, pool_scale, ffn_w_up, ffn_conv_w, ffn_conv_b, ffn_w_down):
    B, S, D = x.shape
    depth = norm_mix_pre.shape[0]
    assert meta_tokens.shape[0] == N_META == HIST
    assert S % ROW_TILE == 0 and S % FFN_TILE == 0 and S % ATT_TILE == 0
    h = x
    hm = jnp.concatenate([meta_tokens.astype(x.dtype),
                          jnp.zeros((META_ROWS - N_META, D), x.dtype)], axis=0)[None]
    tabs_m = _rope_tables(jnp.arange(META_ROWS))
    tabs_t = _rope_tables(N_META + jnp.arange(S))
    row = lambda v: v.reshape(1, -1)
    for i in range(depth):
        j = i // 2
        g_pre, g_post = row(norm_mix_pre[i]), row(norm_mix_post[i])
        if i % 2 == 0:
            w = _mla_weights(mla_w_dqkv[j], mla_w_uq[j], mla_w_ukv[j])
            qn, kvn = row(mla_q_norm[j]), row(mla_kv_norm[j])
            wo = mla_w_o[j].astype(BF16)
            qm, km, vtm = _mla_front(hm, META_ROWS, g_pre, w[0], qn, kvn, *w[1:], tabs_m)
            q, k, vt = _mla_front(h, ATT_TILE, g_pre, w[0], qn, kvn, *w[1:], tabs_t)
            hm = _attention(hm, qm, km, vtm, km, vtm, wo, g_post, tq=META_ROWS,
                            has_meta=False, split_diag=False)
            h = _attention(h, q, k, vt, km, vtm, wo, g_post, tq=ATT_TILE,
                           has_meta=True, split_diag=True)
        else:
            wg = pool_w_group[j].astype(BF16)
            sc = row(pool_scale[j])
            hm_new = _pool_mixer(hm, hm, META_ROWS, 0, False, g_pre, wg, sc, g_post)
            h = _pool_mixer(h, hm, ROW_TILE, N_META, True, g_pre, wg, sc, g_post)
            hm = hm_new
        ffn_w = (row(norm_ffn_pre[i]), ffn_w_up[i].astype(BF16), ffn_conv_w[i],
                 row(ffn_conv_b[i]), ffn_w_down[i].astype(BF16), row(norm_ffn_post[i]))
        hm_new = _conv_ffn(hm, hm, META_ROWS, False, *ffn_w)
        h = _conv_ffn(h, hm, FFN_TILE, True, *ffn_w)
        hm = hm_new
    return h
```

```python
import functools
import math

import jax
import jax.numpy as jnp
from jax import lax
from jax.experimental import pallas as pl
from jax.experimental.pallas import tpu as pltpu

N_META = 16
N_HEADS = 16
QK_NOPE = 64
QK_ROPE = 32
QK_HEAD = QK_NOPE + QK_ROPE
V_HEAD = 64
Q_RANK = 384
KV_RANK = 256
ROPE_THETA = 10000.0
POOL_WINDOWS = (2, 4, 8, 16)
POOL_SHIFT_LEVELS = 3
NORM_EPS = 1e-6

LANES = 128
SUBLANES = 8
BF16_ROWS = 16
SLOT = LANES
V_ROWS = V_HEAD + BF16_ROWS
VMEM_LIMIT = 56 * 1024 * 1024

META_ROWS = 128
HIST = 16
ROW_TILE = 512
FFN_TILE = 1024
ATT_TILE = 512
FF_CHUNK = 256
CONV_BUFS = 4
AHEAD = 4
AHEAD_FULL = 2
NEG_BIG = -1e30

BF16 = jnp.bfloat16
F32 = jnp.float32
NT = (((1,), (1,)), ((), ()))
TN = (((0,), (0,)), ((), ()))


def _rmsnorm(x, g):
    ms = jnp.mean(x * x, axis=-1, keepdims=True)
    return x * lax.rsqrt(ms + NORM_EPS) * g


def _dot(a, b):
    return jnp.dot(a, b, preferred_element_type=F32)


def _const_spec(shape):
    nd = len(shape)
    return pl.BlockSpec(shape, lambda *_: (0,) * nd, pipeline_mode=pl.Buffered(1))


def _params():
    return pltpu.CompilerParams(
        dimension_semantics=("arbitrary", "arbitrary"),
        vmem_limit_bytes=VMEM_LIMIT,
    )


def _history_specs(tm, d):
    per_tile = tm // HIST
    prev = pl.BlockSpec((1, HIST, d), lambda b, t: (b, jnp.maximum(t * per_tile - 1, 0), 0))
    meta = pl.BlockSpec((1, HIST, d), lambda b, t: (0, 0, 0))
    return prev, meta


def _with_history(x, prev_ref, meta_ref, has_hist):
    if has_hist:
        prev = jnp.where(pl.program_id(1) == 0, meta_ref[0], prev_ref[0])
    else:
        prev = jnp.zeros((HIST, x.shape[1]), F32)
    return jnp.concatenate([prev, x], axis=0)


def _mla_front_kernel(h_ref, g_ref, wa_ref, qn_ref, kvn_ref, wqm_ref, wqr_ref,
                      wuk_ref, wuvt_ref, cq_ref, sq_ref, ck_ref, sk_ref,
                      q_out, k_out, vt_out):
    tm = h_ref.shape[1]
    a = _rmsnorm(h_ref[0], g_ref[...]).astype(BF16)
    y = _dot(a, wa_ref[...])
    c_q = _rmsnorm(y[:, :Q_RANK], qn_ref[...]).astype(BF16)
    c_kv = _rmsnorm(y[:, Q_RANK:Q_RANK + KV_RANK], kvn_ref[...]).astype(BF16)
    o = Q_RANK + KV_RANK
    k_rope = y[:, o:o + SLOT] * ck_ref[...] + y[:, o + SLOT:o + 2 * SLOT] * sk_ref[...]
    cq2 = jnp.concatenate([cq_ref[...], cq_ref[...]], axis=1)
    sq2 = jnp.concatenate([sq_ref[...], sq_ref[...]], axis=1)
    kr2 = jnp.concatenate([k_rope, k_rope], axis=1)
    rot_all = _dot(c_q, wqr_ref[...])
    heads_per_blk = SLOT // QK_ROPE

    def rot_slot(hd):
        blk = rot_all[:, SLOT * (hd // heads_per_blk):SLOT * (hd // heads_per_blk + 1)]
        shift = (QK_NOPE - QK_ROPE * (hd % heads_per_blk)) % SLOT
        return blk if shift == 0 else pltpu.roll(blk, shift, axis=1)

    for hp in range(N_HEADS // 2):
        sl = slice(2 * SLOT * hp, 2 * SLOT * (hp + 1))
        rot = jnp.concatenate([rot_slot(2 * hp), rot_slot(2 * hp + 1)], axis=1)
        q = _dot(c_q, wqm_ref[:, sl]) * cq2 + rot * sq2
        k = _dot(c_kv, wuk_ref[:, sl]) + kr2
        for e in range(2):
            q_out[0, 2 * hp + e] = q[:, e * SLOT:(e + 1) * SLOT].astype(BF16)
            k_out[0, 2 * hp + e] = k[:, e * SLOT:(e + 1) * SLOT].astype(BF16)
    ones_row = (lax.broadcasted_iota(jnp.int32, (V_ROWS, tm), 0) == V_HEAD).astype(F32)
    for hd in range(N_HEADS):
        vt = lax.dot_general(wuvt_ref[hd], c_kv, NT, preferred_element_type=F32)
        vt_out[0, hd, 0] = (vt + ones_row).astype(BF16)


def _mla_front(h, tm, g, wa, qn, kvn, wqm, wqr, wuk, wuvt, tabs):
    B, L, D = h.shape
    tab = pl.BlockSpec((tm, SLOT), lambda b, t: (t, 0))
    head_out = pl.BlockSpec((1, N_HEADS, tm, SLOT), lambda b, t: (b, 0, t, 0))
    qk_sds = jax.ShapeDtypeStruct((B, N_HEADS, L, SLOT), BF16)
    vt_sds = jax.ShapeDtypeStruct((B, N_HEADS, L // tm, V_ROWS, tm), BF16)
    return pl.pallas_call(
        _mla_front_kernel,
        grid=(B, L // tm),
        in_specs=[
            pl.BlockSpec((1, tm, D), lambda b, t: (b, t, 0)),
            _const_spec(g.shape), _const_spec(wa.shape), _const_spec(qn.shape),
            _const_spec(kvn.shape), _const_spec(wqm.shape), _const_spec(wqr.shape),
            _const_spec(wuk.shape), _const_spec(wuvt.shape),
            tab, tab, tab, tab,
        ],
        out_specs=[head_out, head_out,
                   pl.BlockSpec((1, N_HEADS, 1, V_ROWS, tm), lambda b, t: (b, 0, t, 0, 0))],
        out_shape=[qk_sds, qk_sds, vt_sds],
        compiler_params=_params(),
        name="mla_front_%d" % tm,
    )(h, g, wa, qn, kvn, wqm, wqr, wuk, wuvt, *tabs)


def _attn_kernel(q_ref, k_ref, vt_ref, km_ref, vtm_ref, h_ref, wo_ref, g_ref, out_ref,
                 m_scr, acc_scr, ot_scr, *, tq, has_meta, split_diag):
    j = pl.program_id(1)

    def scores(keys, hd):
        return lax.dot_general(keys, q_ref[0, hd], NT, preferred_element_type=F32)

    kd = tq // 2 if split_diag else tq
    diag0 = pl.multiple_of(j * tq, tq)
    causal_a = (lax.broadcasted_iota(jnp.int32, (kd, tq), 0)
                <= lax.broadcasted_iota(jnp.int32, (kd, tq), 1))
    causal_b = causal_a[:, :kd]

    def first_scores(hd):
        s_a = jnp.where(causal_a, scores(k_ref[0, hd, pl.ds(diag0, kd), :], hd), NEG_BIG)
        s_b = s_meta = None
        if split_diag:
            s_b = lax.dot_general(k_ref[0, hd, pl.ds(diag0 + kd, kd), :], q_ref[0, hd, kd:, :],
                                  NT, preferred_element_type=F32)
            s_b = jnp.where(causal_b, s_b, NEG_BIG)
        if has_meta:
            s_meta = scores(km_ref[0, hd, :N_META, :], hd)
        return s_a, s_b, s_meta

    def first(hd, s_all):
        s_a, s_b, s_meta = s_all
        m = jnp.max(s_a, axis=0, keepdims=True)
        if has_meta:
            m = jnp.maximum(m, jnp.max(s_meta, axis=0, keepdims=True))
        if split_diag:
            m_hi = jnp.maximum(m[:, kd:], jnp.max(s_b, axis=0, keepdims=True))
            m = jnp.concatenate([m[:, :kd], m_hi], axis=1)
        vt = vt_ref[0, hd, j]
        acc = _dot(vt[:, :kd], jnp.exp2(s_a - m).astype(BF16))
        if has_meta:
            p_meta = jnp.concatenate([jnp.exp2(s_meta - m).astype(BF16),
                                      jnp.zeros((META_ROWS - N_META, tq), BF16)], axis=0)
            acc = acc + _dot(vtm_ref[0, hd, 0], p_meta)
        if split_diag:
            acc_hi = acc[:, kd:] + _dot(vt[:, kd:], jnp.exp2(s_b - m_hi).astype(BF16))
            acc = jnp.concatenate([acc[:, :kd], acc_hi], axis=1)
        acc_scr[hd] = acc
        m_scr[hd] = m

    def update(hd, s, vt):
        m_old = m_scr[hd]
        m_new = jnp.maximum(m_old, jnp.max(s, axis=0, keepdims=True))
        alpha = jnp.exp2(m_old - m_new)
        p = jnp.exp2(s - m_new).astype(BF16)
        acc_scr[hd] = alpha * acc_scr[hd] + _dot(vt, p)
        m_scr[hd] = m_new

    def all_heads(score_fn, step_fn, ahead):
        pending = [score_fn(hd) for hd in range(ahead)]
        for hd in range(N_HEADS):
            if hd + ahead < N_HEADS:
                pending.append(score_fn(hd + ahead))
            step_fn(hd, pending.pop(0))

    all_heads(first_scores, first, AHEAD)

    def full_chunk(c, carry):
        k0 = pl.multiple_of(c * tq, tq)
        all_heads(lambda hd: scores(k_ref[0, hd, pl.ds(k0, tq), :], hd),
                  lambda hd, s: update(hd, s, vt_ref[0, hd, c]), AHEAD_FULL)
        return carry

    lax.fori_loop(0, j, full_chunk, 0)

    for hd in range(N_HEADS):
        acc = acc_scr[hd]
        o = acc[:V_HEAD] * (1.0 / acc[V_HEAD:V_HEAD + 1])
        ot_scr[V_HEAD * hd:V_HEAD * (hd + 1), :] = o.astype(BF16)
    mix = lax.dot_general(ot_scr[...], wo_ref[...], TN, preferred_element_type=F32)
    out_ref[0] = h_ref[0] + _rmsnorm(mix, g_ref[...])


def _attention(h, q, k, vt, k_meta, vt_meta, wo, g, *, tq, has_meta, split_diag):
    B, L, D = h.shape
    row = lambda b, j: (b, j, 0)
    whole = lambda a: pl.BlockSpec((1,) + a.shape[1:], lambda b, j: (b,) + (0,) * (a.ndim - 1))
    shared = lambda a: pl.BlockSpec((1,) + a.shape[1:], lambda b, j: (0,) * a.ndim)
    return pl.pallas_call(
        functools.partial(_attn_kernel, tq=tq, has_meta=has_meta, split_diag=split_diag),
        grid=(B, L // tq),
        in_specs=[
            pl.BlockSpec((1, N_HEADS, tq, SLOT), lambda b, j: (b, 0, j, 0)),
            whole(k), whole(vt), shared(k_meta), shared(vt_meta),
            pl.BlockSpec((1, tq, D), row),
            _const_spec(wo.shape), _const_spec(g.shape),
        ],
        out_specs=pl.BlockSpec((1, tq, D), row),
        out_shape=jax.ShapeDtypeStruct(h.shape, h.dtype),
        scratch_shapes=[
            pltpu.VMEM((N_HEADS, 1, tq), F32),
            pltpu.VMEM((N_HEADS, V_ROWS, tq), F32),
            pltpu.VMEM((N_HEADS * V_HEAD, tq), BF16),
        ],
        compiler_params=_params(),
        name="mla_attn_%d" % tq,
    )(q, k, vt, k_meta, vt_meta, h, wo, g)


def _pool_kernel(h_ref, prev_ref, meta_ref, gpre_ref, wg_ref, scale_ref, gpost_ref, out_ref,
                 shift_scr, *, pos0, has_hist):
    tm = h_ref.shape[1]
    x = h_ref[0]
    a = _rmsnorm(_with_history(x, prev_ref, meta_ref, has_hist), gpre_ref[...])
    n = a.shape[0]
    gd = x.shape[1] // len(POOL_WINDOWS)
    if pos0 + 1 >= max(POOL_WINDOWS):
        inv_cnt = [1.0 / w for w in POOL_WINDOWS]
    else:
        pos1 = lax.broadcasted_iota(jnp.int32, (tm, LANES), 0) + (pl.program_id(1) * tm + pos0 + 1)
        inv_cnt = [1.0 / jnp.minimum(pos1.astype(F32), float(w)) for w in POOL_WINDOWS]
    shift_scr[:, :, 0:SUBLANES, :] = jnp.zeros((shift_scr.shape[0], shift_scr.shape[1],
                                                SUBLANES, LANES), F32)

    def shifted(level, blk, v, span):
        if span % SUBLANES == 0:
            return jnp.concatenate([jnp.zeros((span, LANES), F32), v[:n - span]], axis=0)
        shift_scr[level, blk, pl.ds(SUBLANES, n), :] = v
        return shift_scr[level, blk, pl.ds(SUBLANES - span, n), :]

    ys = []
    for g, w in enumerate(POOL_WINDOWS):
        parts = []
        for blk in range(g * gd // LANES, (g + 1) * gd // LANES):
            a_b = a[:, blk * LANES:(blk + 1) * LANES]
            s, span, level = a_b, 1, 0
            while span < w:
                s = s + shifted(level, blk, s, span)
                span *= 2
                level += 1
            parts.append(s[HIST:, :] * inv_cnt[g] - a_b[HIST:, :])
        ys.append(_dot(jnp.concatenate(parts, axis=1).astype(BF16), wg_ref[g]))
    y = jnp.concatenate(ys, axis=1) * scale_ref[...]
    out_ref[0] = x + _rmsnorm(y, gpost_ref[...])


def _pool_mixer(h, meta_h, tm, pos0, has_hist, gpre, wg, scale, gpost):
    B, L, D = h.shape
    row = lambda b, t: (b, t, 0)
    prev, meta = _history_specs(tm, D)
    return pl.pallas_call(
        functools.partial(_pool_kernel, pos0=pos0, has_hist=has_hist),
        grid=(B, L // tm),
        in_specs=[
            pl.BlockSpec((1, tm, D), row), prev, meta,
            _const_spec(gpre.shape), _const_spec(wg.shape),
            _const_spec(scale.shape), _const_spec(gpost.shape),
        ],
        out_specs=pl.BlockSpec((1, tm, D), row),
        out_shape=jax.ShapeDtypeStruct(h.shape, h.dtype),
        scratch_shapes=[pltpu.VMEM((POOL_SHIFT_LEVELS, D // LANES, SUBLANES + HIST + tm, LANES), F32)],
        compiler_params=_params(),
        name="pool_mixer_%d" % tm,
    )(h, h, meta_h, gpre, wg, scale, gpost)


def _ffn_kernel(h_ref, prev_ref, meta_ref, gpre_ref, wup_ref, cw_ref, cb_ref, wdn_ref,
                gpost_ref, out_ref, act_scr, conv_scr, *, has_hist):
    d_ff = wdn_ref.shape[0]
    tm = h_ref.shape[1]
    x = h_ref[0]
    hn = _rmsnorm(_with_history(x, prev_ref, meta_ref, has_hist), gpre_ref[...]).astype(BF16)

    def conv(col0, buf):
        u = _dot(hn, wup_ref[:, col0:col0 + FF_CHUNK])
        parts = []
        for blk in range(FF_CHUNK // LANES):
            sl = slice(col0 + blk * LANES, col0 + (blk + 1) * LANES)
            u_b = u[:, blk * LANES:(blk + 1) * LANES]
            conv_scr[buf, blk] = u_b
            parts.append(cb_ref[:, sl] + cw_ref[2:3, sl] * u_b[HIST:, :]
                         + cw_ref[1:2, sl] * conv_scr[buf, blk, pl.ds(HIST - 1, tm), :]
                         + cw_ref[0:1, sl] * conv_scr[buf, blk, pl.ds(HIST - 2, tm), :])
        return jnp.concatenate(parts, axis=1)

    for c in range(d_ff // FF_CHUNK):
        gate = conv(c * FF_CHUNK, (2 * c) % CONV_BUFS)
        val = conv(d_ff + c * FF_CHUNK, (2 * c + 1) % CONV_BUFS)
        act = gate * (1.0 / (1.0 + jnp.exp(-gate))) * val
        act_scr[:, c * FF_CHUNK:(c + 1) * FF_CHUNK] = act.astype(BF16)
    f = _dot(act_scr[...], wdn_ref[...])
    out_ref[0] = x + _rmsnorm(f, gpost_ref[...])


def _conv_ffn(h, meta_h, tm, has_hist, gpre, wup, cw, cb, wdn, gpost):
    B, L, D = h.shape
    d_ff = wdn.shape[0]
    assert d_ff % FF_CHUNK == 0
    row = lambda b, t: (b, t, 0)
    prev, meta = _history_specs(tm, D)
    return pl.pallas_call(
        functools.partial(_ffn_kernel, has_hist=has_hist),
        grid=(B, L // tm),
        in_specs=[
            pl.BlockSpec((1, tm, D), row), prev, meta,
            _const_spec(gpre.shape), _const_spec(wup.shape), _const_spec(cw.shape),
            _const_spec(cb.shape), _const_spec(wdn.shape), _const_spec(gpost.shape),
        ],
        out_specs=pl.BlockSpec((1, tm, D), row),
        out_shape=jax.ShapeDtypeStruct(h.shape, h.dtype),
        scratch_shapes=[
            pltpu.VMEM((tm, d_ff), BF16),
            pltpu.VMEM((CONV_BUFS, FF_CHUNK // LANES, HIST + tm, LANES), F32),
        ],
        compiler_params=_params(),
        name="conv_ffn_%d" % tm,
    )(h, h, meta_h, gpre, wup, cw, cb, wdn, gpost)


def _rope_tables(pos):
    n = pos.shape[0]
    inv = 1.0 / (ROPE_THETA ** (jnp.arange(0, QK_ROPE, 2, dtype=F32) / QK_ROPE))
    ang = pos.astype(F32)[:, None] * inv[None, :]
    cos, sin = jnp.cos(ang), jnp.sin(ang)
    z_nope = jnp.zeros((n, QK_NOPE), F32)
    z_pad = jnp.zeros((n, SLOT - QK_HEAD), F32)
    c_k = jnp.concatenate([z_nope, cos, cos, z_pad], axis=1)
    s_k = jnp.concatenate([z_nope, sin, sin, z_pad], axis=1)
    qs = (QK_HEAD ** -0.5) * math.log2(math.e)
    c_q = jnp.concatenate([jnp.ones((n, QK_NOPE), F32), cos, cos, z_pad], axis=1) * qs
    return c_q, s_k * qs, c_k, s_k


def _rot_half(w):
    half = QK_ROPE // 2
    return jnp.concatenate([-w[..., half:], w[..., :half]], axis=-1)


def _mla_weights(w_dqkv, w_uq, w_ukv):
    d = w_dqkv.shape[0]
    o = Q_RANK + KV_RANK
    wkr = w_dqkv[:, o:]
    z = lambda *s: jnp.zeros(s, F32)
    pad = SLOT - QK_HEAD
    wa = jnp.concatenate([
        w_dqkv[:, :o],
        z(d, QK_NOPE), wkr, z(d, pad),
        z(d, QK_NOPE), _rot_half(wkr), z(d, pad)], axis=1).astype(BF16)
    wq = w_uq.reshape(Q_RANK, N_HEADS, QK_HEAD)
    nope, rope = wq[..., :QK_NOPE], wq[..., QK_NOPE:]
    wqm = jnp.concatenate([nope, rope, z(Q_RANK, N_HEADS, pad)], axis=-1)
    wqr = _rot_half(rope).reshape(Q_RANK, N_HEADS * QK_ROPE).astype(BF16)
    wkv = w_ukv.reshape(KV_RANK, N_HEADS, QK_NOPE + V_HEAD)
    wuk = jnp.concatenate([wkv[..., :QK_NOPE], z(KV_RANK, N_HEADS, SLOT - QK_NOPE)], axis=-1)
    wuvt = jnp.concatenate([jnp.transpose(wkv[..., QK_NOPE:], (1, 2, 0)),
                            z(N_HEADS, V_ROWS - V_HEAD, KV_RANK)], axis=1).astype(BF16)
    flat = lambda w: w.reshape(w.shape[0], N_HEADS * SLOT).astype(BF16)
    return wa, flat(wqm), wqr, flat(wuk), wuvt


def kernel(x, meta_tokens, norm_mix_pre, norm_mix_post, norm_ffn_pre, norm_ffn_post,
           mla_w_dqkv, mla_q_norm, mla_w_uq, mla_kv_norm, mla_w_ukv, mla_w_o,
           pool_w_group, pool_scale, ffn_w_up, ffn_conv_w, ffn_conv_b, ffn_w_down):
    B, S, D = x.shape
    depth = norm_mix_pre.shape[0]
    assert meta_tokens.shape[0] == N_META == HIST
    assert S % ROW_TILE == 0 and S % FFN_TILE == 0 and S % ATT_TILE == 0
    h = x
    hm = jnp.concatenate([meta_tokens.astype(x.dtype),
                          jnp.zeros((META_ROWS - N_META, D), x.dtype)], axis=0)[None]
    tabs_m = _rope_tables(jnp.arange(META_ROWS))
    tabs_t = _rope_tables(N_META + jnp.arange(S))
    row = lambda v: v.reshape(1, -1)
    for i in range(depth):
        j = i // 2
        g_pre, g_post = row(norm_mix_pre[i]), row(norm_mix_post[i])
        if i % 2 == 0:
            w = _mla_weights(mla_w_dqkv[j], mla_w_uq[j], mla_w_ukv[j])
            qn, kvn = row(mla_q_norm[j]), row(mla_kv_norm[j])
            wo = mla_w_o[j].astype(BF16)
            qm, km, vtm = _mla_front(hm, META_ROWS, g_pre, w[0], qn, kvn, *w[1:], tabs_m)
            q, k, vt = _mla_front(h, ATT_TILE, g_pre, w[0], qn, kvn, *w[1:], tabs_t)
            hm = _attention(hm, qm, km, vtm, km, vtm, wo, g_post, tq=META_ROWS,
                            has_meta=False, split_diag=False)
            h = _attention(h, q, k, vt, km, vtm, wo, g_post, tq=ATT_TILE,
                           has_meta=True, split_diag=True)
        else:
            wg = pool_w_group[j].astype(BF16)
            sc = row(pool_scale[j])
            hm_new = _pool_mixer(hm, hm, META_ROWS, 0, False, g_pre, wg, sc, g_post)
            h = _pool_mixer(h, hm, ROW_TILE, N_META, True, g_pre, wg, sc, g_post)
            hm = hm_new
        ffn_w = (row(norm_ffn_pre[i]), ffn_w_up[i].astype(BF16), ffn_conv_w[i],
                 row(ffn_conv_b[i]), ffn_w_down[i].astype(BF16), row(norm_ffn_post[i]))
        hm_new = _conv_ffn(hm, hm, META_ROWS, False, *ffn_w)
        h = _conv_ffn(h, hm, FFN_TILE, True, *ffn_w)
        hm = hm_new
    return h
```

```python
import functools
import math

import jax
import jax.numpy as jnp
from jax import lax
from jax.experimental import pallas as pl
from jax.experimental.pallas import tpu as pltpu

N_META = 16
N_HEADS = 16
QK_NOPE = 64
QK_ROPE = 32
QK_HEAD = QK_NOPE + QK_ROPE
V_HEAD = 64
Q_RANK = 384
KV_RANK = 256
ROPE_THETA = 10000.0
POOL_WINDOWS = (2, 4, 8, 16)
POOL_SHIFT_LEVELS = 3
NORM_EPS = 1e-6

LANES = 128
SUBLANES = 8
BF16_ROWS = 16
SLOT = LANES
V_ROWS = V_HEAD + BF16_ROWS
VMEM_LIMIT = 56 * 1024 * 1024

META_ROWS = 128
HIST = 16
FFN_TILE = 1024
POOL_BUFS = 2
ATT_TILE = 512
FF_CHUNK = 256
CONV_BUFS = 4
AHEAD = 4
AHEAD_FULL = 2
NEG_BIG = -1e30

BF16 = jnp.bfloat16
F32 = jnp.float32
NT = (((1,), (1,)), ((), ()))
TN = (((0,), (0,)), ((), ()))


def _rmsnorm(x, g):
    ms = jnp.mean(x * x, axis=-1, keepdims=True)
    return x * lax.rsqrt(ms + NORM_EPS) * g


def _dot(a, b):
    return jnp.dot(a, b, preferred_element_type=F32)


def _const_spec(shape):
    nd = len(shape)
    return pl.BlockSpec(shape, lambda *_: (0,) * nd, pipeline_mode=pl.Buffered(1))


def _params():
    return pltpu.CompilerParams(
        dimension_semantics=("arbitrary", "arbitrary"),
        vmem_limit_bytes=VMEM_LIMIT,
    )


def _history_specs(tm, d):
    per_tile = tm // HIST
    prev = pl.BlockSpec((1, HIST, d), lambda b, t: (b, jnp.maximum(t * per_tile - 1, 0), 0))
    meta = pl.BlockSpec((1, HIST, d), lambda b, t: (0, 0, 0))
    return prev, meta


def _history(prev_ref, meta_ref, has_hist):
    if has_hist:
        return jnp.where(pl.program_id(1) == 0, meta_ref[0], prev_ref[0])
    return jnp.zeros(meta_ref.shape[1:], F32)


def _mla_front_kernel(h_ref, g_ref, wa_ref, qn_ref, kvn_ref, wqm_ref, wqr_ref,
                      wuk_ref, wuvt_ref, cq_ref, sq_ref, ck_ref, sk_ref,
                      q_out, k_out, vt_out):
    tm = h_ref.shape[1]
    a = _rmsnorm(h_ref[0], g_ref[...]).astype(BF16)
    y = _dot(a, wa_ref[...])
    c_q = _rmsnorm(y[:, :Q_RANK], qn_ref[...]).astype(BF16)
    c_kv = _rmsnorm(y[:, Q_RANK:Q_RANK + KV_RANK], kvn_ref[...]).astype(BF16)
    o = Q_RANK + KV_RANK
    k_rope = y[:, o:o + SLOT] * ck_ref[...] + y[:, o + SLOT:o + 2 * SLOT] * sk_ref[...]
    cq2 = jnp.concatenate([cq_ref[...], cq_ref[...]], axis=1)
    sq2 = jnp.concatenate([sq_ref[...], sq_ref[...]], axis=1)
    kr2 = jnp.concatenate([k_rope, k_rope], axis=1)
    rot_all = _dot(c_q, wqr_ref[...])
    heads_per_blk = SLOT // QK_ROPE

    def rot_slot(hd):
        blk = rot_all[:, SLOT * (hd // heads_per_blk):SLOT * (hd // heads_per_blk + 1)]
        shift = (QK_NOPE - QK_ROPE * (hd % heads_per_blk)) % SLOT
        return blk if shift == 0 else pltpu.roll(blk, shift, axis=1)

    for hp in range(N_HEADS // 2):
        sl = slice(2 * SLOT * hp, 2 * SLOT * (hp + 1))
        rot = jnp.concatenate([rot_slot(2 * hp), rot_slot(2 * hp + 1)], axis=1)
        q = _dot(c_q, wqm_ref[:, sl]) * cq2 + rot * sq2
        k = _dot(c_kv, wuk_ref[:, sl]) + kr2
        for e in range(2):
            q_out[0, 2 * hp + e] = q[:, e * SLOT:(e + 1) * SLOT].astype(BF16)
            k_out[0, 2 * hp + e] = k[:, e * SLOT:(e + 1) * SLOT].astype(BF16)
    ones_row = (lax.broadcasted_iota(jnp.int32, (V_ROWS, tm), 0) == V_HEAD).astype(F32)
    for hd in range(N_HEADS):
        vt = lax.dot_general(wuvt_ref[hd], c_kv, NT, preferred_element_type=F32)
        vt_out[0, hd, 0] = (vt + ones_row).astype(BF16)


def _mla_front(h, tm, g, wa, qn, kvn, wqm, wqr, wuk, wuvt, tabs):
    B, L, D = h.shape
    tab = pl.BlockSpec((tm, SLOT), lambda b, t: (t, 0))
    head_out = pl.BlockSpec((1, N_HEADS, tm, SLOT), lambda b, t: (b, 0, t, 0))
    qk_sds = jax.ShapeDtypeStruct((B, N_HEADS, L, SLOT), BF16)
    vt_sds = jax.ShapeDtypeStruct((B, N_HEADS, L // tm, V_ROWS, tm), BF16)
    return pl.pallas_call(
        _mla_front_kernel,
        grid=(B, L // tm),
        in_specs=[
            pl.BlockSpec((1, tm, D), lambda b, t: (b, t, 0)),
            _const_spec(g.shape), _const_spec(wa.shape), _const_spec(qn.shape),
            _const_spec(kvn.shape), _const_spec(wqm.shape), _const_spec(wqr.shape),
            _const_spec(wuk.shape), _const_spec(wuvt.shape),
            tab, tab, tab, tab,
        ],
        out_specs=[head_out, head_out,
                   pl.BlockSpec((1, N_HEADS, 1, V_ROWS, tm), lambda b, t: (b, 0, t, 0, 0))],
        out_shape=[qk_sds, qk_sds, vt_sds],
        compiler_params=_params(),
        name="mla_front_%d" % tm,
    )(h, g, wa, qn, kvn, wqm, wqr, wuk, wuvt, *tabs)


def _attn_kernel(q_ref, k_ref, vt_ref, km_ref, vtm_ref, h_ref, wo_ref, g_ref, out_ref,
                 m_scr, acc_scr, ot_scr, *, tq, has_meta, split_diag):
    j = pl.program_id(1)

    def scores(keys, hd):
        return lax.dot_general(keys, q_ref[0, hd], NT, preferred_element_type=F32)

    kd = tq // 2 if split_diag else tq
    diag0 = pl.multiple_of(j * tq, tq)
    causal_a = (lax.broadcasted_iota(jnp.int32, (kd, tq), 0)
                <= lax.broadcasted_iota(jnp.int32, (kd, tq), 1))
    causal_b = causal_a[:, :kd]

    def first_scores(hd):
        s_a = jnp.where(causal_a, scores(k_ref[0, hd, pl.ds(diag0, kd), :], hd), NEG_BIG)
        s_b = s_meta = None
        if split_diag:
            s_b = lax.dot_general(k_ref[0, hd, pl.ds(diag0 + kd, kd), :], q_ref[0, hd, kd:, :],
                                  NT, preferred_element_type=F32)
            s_b = jnp.where(causal_b, s_b, NEG_BIG)
        if has_meta:
            s_meta = scores(km_ref[0, hd, :N_META, :], hd)
        return s_a, s_b, s_meta

    def first(hd, s_all):
        s_a, s_b, s_meta = s_all
        m = jnp.max(s_a, axis=0, keepdims=True)
        if has_meta:
            m = jnp.maximum(m, jnp.max(s_meta, axis=0, keepdims=True))
        if split_diag:
            m_hi = jnp.maximum(m[:, kd:], jnp.max(s_b, axis=0, keepdims=True))
            m = jnp.concatenate([m[:, :kd], m_hi], axis=1)
        vt = vt_ref[0, hd, j]
        acc = _dot(vt[:, :kd], jnp.exp2(s_a - m).astype(BF16))
        if has_meta:
            p_meta = jnp.concatenate([jnp.exp2(s_meta - m).astype(BF16),
                                      jnp.zeros((META_ROWS - N_META, tq), BF16)], axis=0)
            acc = acc + _dot(vtm_ref[0, hd, 0], p_meta)
        if split_diag:
            acc_hi = acc[:, kd:] + _dot(vt[:, kd:], jnp.exp2(s_b - m_hi).astype(BF16))
            acc = jnp.concatenate([acc[:, :kd], acc_hi], axis=1)
        acc_scr[hd] = acc
        m_scr[hd] = m

    def update(hd, s, vt):
        m_old = m_scr[hd]
        m_new = jnp.maximum(m_old, jnp.max(s, axis=0, keepdims=True))
        alpha = jnp.exp2(m_old - m_new)
        p = jnp.exp2(s - m_new).astype(BF16)
        acc_scr[hd] = alpha * acc_scr[hd] + _dot(vt, p)
        m_scr[hd] = m_new

    def all_heads(score_fn, step_fn, ahead):
        pending = [score_fn(hd) for hd in range(ahead)]
        for hd in range(N_HEADS):
            if hd + ahead < N_HEADS:
                pending.append(score_fn(hd + ahead))
            step_fn(hd, pending.pop(0))

    all_heads(first_scores, first, AHEAD)

    def full_chunk(c, carry):
        k0 = pl.multiple_of(c * tq, tq)
        all_heads(lambda hd: scores(k_ref[0, hd, pl.ds(k0, tq), :], hd),
                  lambda hd, s: update(hd, s, vt_ref[0, hd, c]), AHEAD_FULL)
        return carry

    lax.fori_loop(0, j, full_chunk, 0)

    for hd in range(N_HEADS):
        acc = acc_scr[hd]
        o = acc[:V_HEAD] * (1.0 / acc[V_HEAD:V_HEAD + 1])
        ot_scr[V_HEAD * hd:V_HEAD * (hd + 1), :] = o.astype(BF16)
    mix = lax.dot_general(ot_scr[...], wo_ref[...], TN, preferred_element_type=F32)
    out_ref[0] = h_ref[0] + _rmsnorm(mix, g_ref[...])


def _attention(h, q, k, vt, k_meta, vt_meta, wo, g, *, tq, has_meta, split_diag):
    B, L, D = h.shape
    row = lambda b, j: (b, j, 0)
    whole = lambda a: pl.BlockSpec((1,) + a.shape[1:], lambda b, j: (b,) + (0,) * (a.ndim - 1))
    shared = lambda a: pl.BlockSpec((1,) + a.shape[1:], lambda b, j: (0,) * a.ndim)
    return pl.pallas_call(
        functools.partial(_attn_kernel, tq=tq, has_meta=has_meta, split_diag=split_diag),
        grid=(B, L // tq),
        in_specs=[
            pl.BlockSpec((1, N_HEADS, tq, SLOT), lambda b, j: (b, 0, j, 0)),
            whole(k), whole(vt), shared(k_meta), shared(vt_meta),
            pl.BlockSpec((1, tq, D), row),
            _const_spec(wo.shape), _const_spec(g.shape),
        ],
        out_specs=pl.BlockSpec((1, tq, D), row),
        out_shape=jax.ShapeDtypeStruct(h.shape, h.dtype),
        scratch_shapes=[
            pltpu.VMEM((N_HEADS, 1, tq), F32),
            pltpu.VMEM((N_HEADS, V_ROWS, tq), F32),
            pltpu.VMEM((N_HEADS * V_HEAD, tq), BF16),
        ],
        compiler_params=_params(),
        name="mla_attn_%d" % tq,
    )(q, k, vt, k_meta, vt_meta, h, wo, g)


def _pool_mix(x, hist, gpre_ref, wg_ref, scale_ref, gpost_ref, shift_scr, pos0):
    tm = x.shape[0]
    a = _rmsnorm(jnp.concatenate([hist, x], axis=0), gpre_ref[...])
    n = a.shape[0]
    n_bufs = shift_scr.shape[1]
    gd = x.shape[1] // len(POOL_WINDOWS)
    if pos0 + 1 >= max(POOL_WINDOWS):
        inv_cnt = [1.0 / w for w in POOL_WINDOWS]
    else:
        pos1 = lax.broadcasted_iota(jnp.int32, (tm, LANES), 0) + (pl.program_id(1) * tm + pos0 + 1)
        inv_cnt = [1.0 / jnp.minimum(pos1.astype(F32), float(w)) for w in POOL_WINDOWS]
    shift_scr[:, :, 0:SUBLANES, :] = jnp.zeros((shift_scr.shape[0], shift_scr.shape[1],
                                                SUBLANES, LANES), F32)

    def shifted(level, blk, v, span):
        if span % SUBLANES == 0:
            return jnp.concatenate([jnp.zeros((span, LANES), F32), v[:n - span]], axis=0)
        shift_scr[level, blk % n_bufs, pl.ds(SUBLANES, n), :] = v
        return shift_scr[level, blk % n_bufs, pl.ds(SUBLANES - span, n), :]

    ys = []
    for g, w in enumerate(POOL_WINDOWS):
        parts = []
        for blk in range(g * gd // LANES, (g + 1) * gd // LANES):
            a_b = a[:, blk * LANES:(blk + 1) * LANES]
            s, span, level = a_b, 1, 0
            while span < w:
                s = s + shifted(level, blk, s, span)
                span *= 2
                level += 1
            parts.append(s[HIST:, :] * inv_cnt[g] - a_b[HIST:, :])
        ys.append(_dot(jnp.concatenate(parts, axis=1).astype(BF16), wg_ref[g]))
    y = jnp.concatenate(ys, axis=1) * scale_ref[...]
    return x + _rmsnorm(y, gpost_ref[...])


def _ffn_mix(x, hist, gpre_ref, wup_ref, cw_ref, cb_ref, wdn_ref, gpost_ref, act_scr, conv_scr):
    d_ff = wdn_ref.shape[0]
    tm = x.shape[0]
    hn = _rmsnorm(jnp.concatenate([hist, x], axis=0), gpre_ref[...]).astype(BF16)

    def conv(col0, buf):
        u = _dot(hn, wup_ref[:, col0:col0 + FF_CHUNK])
        parts = []
        for blk in range(FF_CHUNK // LANES):
            sl = slice(col0 + blk * LANES, col0 + (blk + 1) * LANES)
            u_b = u[:, blk * LANES:(blk + 1) * LANES]
            conv_scr[buf, blk] = u_b
            parts.append(cb_ref[:, sl] + cw_ref[2:3, sl] * u_b[HIST:, :]
                         + cw_ref[1:2, sl] * conv_scr[buf, blk, pl.ds(HIST - 1, tm), :]
                         + cw_ref[0:1, sl] * conv_scr[buf, blk, pl.ds(HIST - 2, tm), :])
        return jnp.concatenate(parts, axis=1)

    for c in range(d_ff // FF_CHUNK):
        gate = conv(c * FF_CHUNK, (2 * c) % CONV_BUFS)
        val = conv(d_ff + c * FF_CHUNK, (2 * c + 1) % CONV_BUFS)
        act = gate * (1.0 / (1.0 + jnp.exp(-gate))) * val
        act_scr[:, c * FF_CHUNK:(c + 1) * FF_CHUNK] = act.astype(BF16)
    f = _dot(act_scr[...], wdn_ref[...])
    return x + _rmsnorm(f, gpost_ref[...])


def _ffn_kernel(h_ref, prev_ref, meta_ref, *rest, has_hist):
    (gpre_ref, wup_ref, cw_ref, cb_ref, wdn_ref, gpost_ref, out_ref, act_scr, conv_scr) = rest
    out_ref[0] = _ffn_mix(h_ref[0], _history(prev_ref, meta_ref, has_hist),
                          gpre_ref, wup_ref, cw_ref, cb_ref, wdn_ref, gpost_ref, act_scr, conv_scr)


def _pool_ffn_kernel(h_ref, prev_ref, meta_ref, meta_mid_ref, pgpre_ref, wg_ref, scale_ref,
                     pgpost_ref, *rest, pos0, has_hist):
    (gpre_ref, wup_ref, cw_ref, cb_ref, wdn_ref, gpost_ref,
     out_ref, mid_ref, act_scr, conv_scr, shift_scr, hist_scr) = rest
    mid = _pool_mix(h_ref[0], _history(prev_ref, meta_ref, has_hist),
                    pgpre_ref, wg_ref, scale_ref, pgpost_ref, shift_scr, pos0)
    mid_ref[0, 0] = mid[:HIST]
    if has_hist:
        @pl.when(pl.program_id(1) == 0)
        def _():
            hist_scr[...] = meta_mid_ref[0]

        hist = hist_scr[...]
        hist_scr[...] = mid[mid.shape[0] - HIST:]
    else:
        hist = jnp.zeros(hist_scr.shape, F32)
    out_ref[0] = _ffn_mix(mid, hist, gpre_ref, wup_ref, cw_ref, cb_ref, wdn_ref, gpost_ref,
                          act_scr, conv_scr)


def _ffn_specs(tm, d_ff, *ffn_w):
    scratch = [pltpu.VMEM((tm, d_ff), BF16),
               pltpu.VMEM((CONV_BUFS, FF_CHUNK // LANES, HIST + tm, LANES), F32)]
    return [_const_spec(w.shape) for w in ffn_w], scratch


def _conv_ffn(h, meta_h, tm, has_hist, *ffn_w):
    B, L, D = h.shape
    d_ff = ffn_w[4].shape[0]
    assert d_ff % FF_CHUNK == 0
    row = lambda b, t: (b, t, 0)
    prev, meta = _history_specs(tm, D)
    w_specs, scratch = _ffn_specs(tm, d_ff, *ffn_w)
    return pl.pallas_call(
        functools.partial(_ffn_kernel, has_hist=has_hist),
        grid=(B, L // tm),
        in_specs=[pl.BlockSpec((1, tm, D), row), prev, meta] + w_specs,
        out_specs=pl.BlockSpec((1, tm, D), row),
        out_shape=jax.ShapeDtypeStruct(h.shape, h.dtype),
        scratch_shapes=scratch,
        compiler_params=_params(),
        name="conv_ffn_%d" % tm,
    )(h, h, meta_h, *ffn_w)


def _pool_ffn(h, meta_h, meta_mid, tm, pos0, has_hist, pool_w, ffn_w):
    B, L, D = h.shape
    d_ff = ffn_w[4].shape[0]
    assert d_ff % FF_CHUNK == 0
    row = lambda b, t: (b, t, 0)
    prev, meta = _history_specs(tm, D)
    w_specs, scratch = _ffn_specs(tm, d_ff, *ffn_w)
    return pl.pallas_call(
        functools.partial(_pool_ffn_kernel, pos0=pos0, has_hist=has_hist),
        grid=(B, L // tm),
        in_specs=([pl.BlockSpec((1, tm, D), row), prev, meta, meta]
                  + [_const_spec(w.shape) for w in pool_w] + w_specs),
        out_specs=[pl.BlockSpec((1, tm, D), row),
                   pl.BlockSpec((1, 1, HIST, D), lambda b, t: (b, t, 0, 0))],
        out_shape=[jax.ShapeDtypeStruct(h.shape, h.dtype),
                   jax.ShapeDtypeStruct((B, L // tm, HIST, D), h.dtype)],
        scratch_shapes=scratch + [
            pltpu.VMEM((POOL_SHIFT_LEVELS, POOL_BUFS, SUBLANES + HIST + tm, LANES), F32),
            pltpu.VMEM((HIST, D), F32),
        ],
        compiler_params=_params(),
        name="pool_ffn_%d" % tm,
    )(h, h, meta_h, meta_mid, *pool_w, *ffn_w)


def _rope_tables(pos):
    n = pos.shape[0]
    inv = 1.0 / (ROPE_THETA ** (jnp.arange(0, QK_ROPE, 2, dtype=F32) / QK_ROPE))
    ang = pos.astype(F32)[:, None] * inv[None, :]
    cos, sin = jnp.cos(ang), jnp.sin(ang)
    z_nope = jnp.zeros((n, QK_NOPE), F32)
    z_pad = jnp.zeros((n, SLOT - QK_HEAD), F32)
    c_k = jnp.concatenate([z_nope, cos, cos, z_pad], axis=1)
    s_k = jnp.concatenate([z_nope, sin, sin, z_pad], axis=1)
    qs = (QK_HEAD ** -0.5) * math.log2(math.e)
    c_q = jnp.concatenate([jnp.ones((n, QK_NOPE), F32), cos, cos, z_pad], axis=1) * qs
    return c_q, s_k * qs, c_k, s_k


def _rot_half(w):
    half = QK_ROPE // 2
    return jnp.concatenate([-w[..., half:], w[..., :half]], axis=-1)


def _mla_weights(w_dqkv, w_uq, w_ukv):
    d = w_dqkv.shape[0]
    o = Q_RANK + KV_RANK
    wkr = w_dqkv[:, o:]
    z = lambda *s: jnp.zeros(s, F32)
    pad = SLOT - QK_HEAD
    wa = jnp.concatenate([
        w_dqkv[:, :o],
        z(d, QK_NOPE), wkr, z(d, pad),
        z(d, QK_NOPE), _rot_half(wkr), z(d, pad)], axis=1).astype(BF16)
    wq = w_uq.reshape(Q_RANK, N_HEADS, QK_HEAD)
    nope, rope = wq[..., :QK_NOPE], wq[..., QK_NOPE:]
    wqm = jnp.concatenate([nope, rope, z(Q_RANK, N_HEADS, pad)], axis=-1)
    wqr = _rot_half(rope).reshape(Q_RANK, N_HEADS * QK_ROPE).astype(BF16)
    wkv = w_ukv.reshape(KV_RANK, N_HEADS, QK_NOPE + V_HEAD)
    wuk = jnp.concatenate([wkv[..., :QK_NOPE], z(KV_RANK, N_HEADS, SLOT - QK_NOPE)], axis=-1)
    wuvt = jnp.concatenate([jnp.transpose(wkv[..., QK_NOPE:], (1, 2, 0)),
                            z(N_HEADS, V_ROWS - V_HEAD, KV_RANK)], axis=1).astype(BF16)
    flat = lambda w: w.reshape(w.shape[0], N_HEADS * SLOT).astype(BF16)
    return wa, flat(wqm), wqr, flat(wuk), wuvt


def kernel(x, meta_tokens, norm_mix_pre, norm_mix_post, norm_ffn_pre, norm_ffn_post,
           mla_w_dqkv, mla_q_norm, mla_w_uq, mla_kv_norm, mla_w_ukv, mla_w_o,
           pool_w_group, pool_scale, ffn_w_up, ffn_conv_w, ffn_conv_b, ffn_w_down):
    B, S, D = x.shape
    depth = norm_mix_pre.shape[0]
    assert meta_tokens.shape[0] == N_META == HIST
    assert S % FFN_TILE == 0 and S % ATT_TILE == 0
    h = x
    hm = jnp.concatenate([meta_tokens.astype(x.dtype),
                          jnp.zeros((META_ROWS - N_META, D), x.dtype)], axis=0)[None]
    tabs_m = _rope_tables(jnp.arange(META_ROWS))
    tabs_t = _rope_tables(N_META + jnp.arange(S))
    row = lambda v: v.reshape(1, -1)
    for i in range(depth):
        j = i // 2
        g_pre, g_post = row(norm_mix_pre[i]), row(norm_mix_post[i])
        ffn_w = (row(norm_ffn_pre[i]), ffn_w_up[i].astype(BF16), ffn_conv_w[i],
                 row(ffn_conv_b[i]), ffn_w_down[i].astype(BF16), row(norm_ffn_post[i]))
        if i % 2 == 0:
            w = _mla_weights(mla_w_dqkv[j], mla_w_uq[j], mla_w_ukv[j])
            qn, kvn = row(mla_q_norm[j]), row(mla_kv_norm[j])
            wo = mla_w_o[j].astype(BF16)
            qm, km, vtm = _mla_front(hm, META_ROWS, g_pre, w[0], qn, kvn, *w[1:], tabs_m)
            q, k, vt = _mla_front(h, ATT_TILE, g_pre, w[0], qn, kvn, *w[1:], tabs_t)
            hm = _attention(hm, qm, km, vtm, km, vtm, wo, g_post, tq=META_ROWS,
                            has_meta=False, split_diag=False)
            h = _attention(h, q, k, vt, km, vtm, wo, g_post, tq=ATT_TILE,
                           has_meta=True, split_diag=True)
            hm_new = _conv_ffn(hm, hm, META_ROWS, False, *ffn_w)
            h = _conv_ffn(h, hm, FFN_TILE, True, *ffn_w)
        else:
            pool_w = (g_pre, pool_w_group[j].astype(BF16), row(pool_scale[j]), g_post)
            hm_new, hm_mid = _pool_ffn(hm, hm, hm, META_ROWS, 0, False, pool_w, ffn_w)
            h, _ = _pool_ffn(h, hm, hm_mid[:, 0], FFN_TILE, N_META, True, pool_w, ffn_w)
        hm = hm_new
    return h
```

```python
import functools
import math

import jax
import jax.numpy as jnp
from jax import lax
from jax.experimental import pallas as pl
from jax.experimental.pallas import tpu as pltpu

N_META = 16
N_HEADS = 16
QK_NOPE = 64
QK_ROPE = 32
QK_HEAD = QK_NOPE + QK_ROPE
V_HEAD = 64
Q_RANK = 384
KV_RANK = 256
ROPE_THETA = 10000.0
POOL_WINDOWS = (2, 4, 8, 16)
POOL_SHIFT_LEVELS = 3
NORM_EPS = 1e-6

LANES = 128
SUBLANES = 8
BF16_ROWS = 16
SLOT = LANES
V_ROWS = V_HEAD + BF16_ROWS
VMEM_LIMIT = 56 * 1024 * 1024

META_ROWS = 128
HIST = 16
FFN_TILE = 1024
POOL_BUFS = 2
ATT_TILE = 512
FF_CHUNK = 256
CONV_BUFS = 4
AHEAD = 4
AHEAD_FULL = 2
NEG_BIG = -1e30

BF16 = jnp.bfloat16
F32 = jnp.float32
NT = (((1,), (1,)), ((), ()))
TN = (((0,), (0,)), ((), ()))


def _rmsnorm(x, g):
    ms = jnp.mean(x * x, axis=-1, keepdims=True)
    return x * lax.rsqrt(ms + NORM_EPS) * g


def _dot(a, b):
    return jnp.dot(a, b, preferred_element_type=F32)


def _const_spec(shape):
    nd = len(shape)
    return pl.BlockSpec(shape, lambda *_: (0,) * nd, pipeline_mode=pl.Buffered(1))


def _params():
    return pltpu.CompilerParams(
        dimension_semantics=("arbitrary", "arbitrary"),
        vmem_limit_bytes=VMEM_LIMIT,
    )


def _history_specs(tm, d):
    per_tile = tm // HIST
    prev = pl.BlockSpec((1, HIST, d), lambda b, t: (b, jnp.maximum(t * per_tile - 1, 0), 0))
    meta = pl.BlockSpec((1, HIST, d), lambda b, t: (0, 0, 0))
    return prev, meta


def _history(prev_ref, meta_ref, has_hist):
    if has_hist:
        return jnp.where(pl.program_id(1) == 0, meta_ref[0], prev_ref[0])
    return jnp.zeros(meta_ref.shape[1:], F32)


def _mla_front_kernel(h_ref, g_ref, wa_ref, qn_ref, kvn_ref, wqm_ref, wqr_ref,
                      wuk_ref, wuvt_ref, cq_ref, sq_ref, ck_ref, sk_ref,
                      q_out, k_out, vt_out):
    tm = h_ref.shape[1]
    a = _rmsnorm(h_ref[0], g_ref[...]).astype(BF16)
    y = _dot(a, wa_ref[...])
    c_q = _rmsnorm(y[:, :Q_RANK], qn_ref[...]).astype(BF16)
    c_kv = _rmsnorm(y[:, Q_RANK:Q_RANK + KV_RANK], kvn_ref[...]).astype(BF16)
    o = Q_RANK + KV_RANK
    k_rope = y[:, o:o + SLOT] * ck_ref[...] + y[:, o + SLOT:o + 2 * SLOT] * sk_ref[...]
    cq2 = jnp.concatenate([cq_ref[...], cq_ref[...]], axis=1)
    sq2 = jnp.concatenate([sq_ref[...], sq_ref[...]], axis=1)
    kr2 = jnp.concatenate([k_rope, k_rope], axis=1)
    rot_all = _dot(c_q, wqr_ref[...])
    heads_per_blk = SLOT // QK_ROPE

    def rot_slot(hd):
        blk = rot_all[:, SLOT * (hd // heads_per_blk):SLOT * (hd // heads_per_blk + 1)]
        shift = (QK_NOPE - QK_ROPE * (hd % heads_per_blk)) % SLOT
        return blk if shift == 0 else pltpu.roll(blk, shift, axis=1)

    for hp in range(N_HEADS // 2):
        sl = slice(2 * SLOT * hp, 2 * SLOT * (hp + 1))
        rot = jnp.concatenate([rot_slot(2 * hp), rot_slot(2 * hp + 1)], axis=1)
        q = _dot(c_q, wqm_ref[:, sl]) * cq2 + rot * sq2
        k = _dot(c_kv, wuk_ref[:, sl]) + kr2
        for e in range(2):
            q_out[0, 2 * hp + e] = q[:, e * SLOT:(e + 1) * SLOT].astype(BF16)
            k_out[0, 2 * hp + e] = k[:, e * SLOT:(e + 1) * SLOT].astype(BF16)
    vt_all = lax.dot_general(wuvt_ref[...], c_kv, NT, preferred_element_type=F32)
    ones_row = (lax.broadcasted_iota(jnp.int32, (V_ROWS, tm), 0) == V_HEAD).astype(F32)
    for hd in range(N_HEADS):
        vt = vt_all[V_ROWS * hd:V_ROWS * (hd + 1), :] + ones_row
        vt_out[0, hd, 0] = vt.astype(BF16)


def _mla_front(h, tm, g, wa, qn, kvn, wqm, wqr, wuk, wuvt, tabs):
    B, L, D = h.shape
    tab = pl.BlockSpec((tm, SLOT), lambda b, t: (t, 0))
    head_out = pl.BlockSpec((1, N_HEADS, tm, SLOT), lambda b, t: (b, 0, t, 0))
    qk_sds = jax.ShapeDtypeStruct((B, N_HEADS, L, SLOT), BF16)
    vt_sds = jax.ShapeDtypeStruct((B, N_HEADS, L // tm, V_ROWS, tm), BF16)
    return pl.pallas_call(
        _mla_front_kernel,
        grid=(B, L // tm),
        in_specs=[
            pl.BlockSpec((1, tm, D), lambda b, t: (b, t, 0)),
            _const_spec(g.shape), _const_spec(wa.shape), _const_spec(qn.shape),
            _const_spec(kvn.shape), _const_spec(wqm.shape), _const_spec(wqr.shape),
            _const_spec(wuk.shape), _const_spec(wuvt.shape),
            tab, tab, tab, tab,
        ],
        out_specs=[head_out, head_out,
                   pl.BlockSpec((1, N_HEADS, 1, V_ROWS, tm), lambda b, t: (b, 0, t, 0, 0))],
        out_shape=[qk_sds, qk_sds, vt_sds],
        compiler_params=_params(),
        name="mla_front_%d" % tm,
    )(h, g, wa, qn, kvn, wqm, wqr, wuk, wuvt, *tabs)


def _attn_kernel(q_ref, k_ref, vt_ref, km_ref, vtm_ref, h_ref, wo_ref, g_ref, out_ref,
                 m_scr, acc_scr, ot_scr, *, tq, has_meta, split_diag):
    j = pl.program_id(1)

    def scores(keys, hd):
        return lax.dot_general(keys, q_ref[0, hd], NT, preferred_element_type=F32)

    kd = tq // 2 if split_diag else tq
    diag0 = pl.multiple_of(j * tq, tq)
    causal_a = (lax.broadcasted_iota(jnp.int32, (kd, tq), 0)
                <= lax.broadcasted_iota(jnp.int32, (kd, tq), 1))
    causal_b = causal_a[:, :kd]

    def first_scores(hd):
        s_a = jnp.where(causal_a, scores(k_ref[0, hd, pl.ds(diag0, kd), :], hd), NEG_BIG)
        s_b = s_meta = None
        if split_diag:
            s_b = lax.dot_general(k_ref[0, hd, pl.ds(diag0 + kd, kd), :], q_ref[0, hd, kd:, :],
                                  NT, preferred_element_type=F32)
            s_b = jnp.where(causal_b, s_b, NEG_BIG)
        if has_meta:
            s_meta = scores(km_ref[0, hd, :N_META, :], hd)
        return s_a, s_b, s_meta

    def first(hd, s_all):
        s_a, s_b, s_meta = s_all
        m = jnp.max(s_a, axis=0, keepdims=True)
        if has_meta:
            m = jnp.maximum(m, jnp.max(s_meta, axis=0, keepdims=True))
        if split_diag:
            m_hi = jnp.maximum(m[:, kd:], jnp.max(s_b, axis=0, keepdims=True))
            m = jnp.concatenate([m[:, :kd], m_hi], axis=1)
        vt = vt_ref[0, hd, j]
        acc = _dot(vt[:, :kd], jnp.exp2(s_a - m).astype(BF16))
        if has_meta:
            p_meta = jnp.concatenate([jnp.exp2(s_meta - m).astype(BF16),
                                      jnp.zeros((META_ROWS - N_META, tq), BF16)], axis=0)
            acc = acc + _dot(vtm_ref[0, hd, 0], p_meta)
        if split_diag:
            acc_hi = acc[:, kd:] + _dot(vt[:, kd:], jnp.exp2(s_b - m_hi).astype(BF16))
            acc = jnp.concatenate([acc[:, :kd], acc_hi], axis=1)
        acc_scr[hd] = acc
        m_scr[hd] = m

    def update(hd, s, vt):
        m_old = m_scr[hd]
        m_new = jnp.maximum(m_old, jnp.max(s, axis=0, keepdims=True))
        alpha = jnp.exp2(m_old - m_new)
        p = jnp.exp2(s - m_new).astype(BF16)
        acc_scr[hd] = alpha * acc_scr[hd] + _dot(vt, p)
        m_scr[hd] = m_new

    def all_heads(score_fn, step_fn, ahead):
        pending = [score_fn(hd) for hd in range(ahead)]
        for hd in range(N_HEADS):
            if hd + ahead < N_HEADS:
                pending.append(score_fn(hd + ahead))
            step_fn(hd, pending.pop(0))

    all_heads(first_scores, first, AHEAD)

    def full_chunk(c, carry):
        k0 = pl.multiple_of(c * tq, tq)
        all_heads(lambda hd: scores(k_ref[0, hd, pl.ds(k0, tq), :], hd),
                  lambda hd, s: update(hd, s, vt_ref[0, hd, c]), AHEAD_FULL)
        return carry

    lax.fori_loop(0, j, full_chunk, 0)

    for hd in range(N_HEADS):
        acc = acc_scr[hd]
        o = acc[:V_HEAD] * (1.0 / acc[V_HEAD:V_HEAD + 1])
        ot_scr[V_HEAD * hd:V_HEAD * (hd + 1), :] = o.astype(BF16)
    mix = lax.dot_general(ot_scr[...], wo_ref[...], TN, preferred_element_type=F32)
    out_ref[0] = h_ref[0] + _rmsnorm(mix, g_ref[...])


def _attention(h, q, k, vt, k_meta, vt_meta, wo, g, *, tq, has_meta, split_diag):
    B, L, D = h.shape
    row = lambda b, j: (b, j, 0)
    whole = lambda a: pl.BlockSpec((1,) + a.shape[1:], lambda b, j: (b,) + (0,) * (a.ndim - 1))
    shared = lambda a: pl.BlockSpec((1,) + a.shape[1:], lambda b, j: (0,) * a.ndim)
    return pl.pallas_call(
        functools.partial(_attn_kernel, tq=tq, has_meta=has_meta, split_diag=split_diag),
        grid=(B, L // tq),
        in_specs=[
            pl.BlockSpec((1, N_HEADS, tq, SLOT), lambda b, j: (b, 0, j, 0)),
            whole(k), whole(vt), shared(k_meta), shared(vt_meta),
            pl.BlockSpec((1, tq, D), row),
            _const_spec(wo.shape), _const_spec(g.shape),
        ],
        out_specs=pl.BlockSpec((1, tq, D), row),
        out_shape=jax.ShapeDtypeStruct(h.shape, h.dtype),
        scratch_shapes=[
            pltpu.VMEM((N_HEADS, 1, tq), F32),
            pltpu.VMEM((N_HEADS, V_ROWS, tq), F32),
            pltpu.VMEM((N_HEADS * V_HEAD, tq), BF16),
        ],
        compiler_params=_params(),
        name="mla_attn_%d" % tq,
    )(q, k, vt, k_meta, vt_meta, h, wo, g)


def _pool_mix(x, hist, gpre_ref, wg_ref, scale_ref, gpost_ref, shift_scr, pos0):
    tm = x.shape[0]
    a = _rmsnorm(jnp.concatenate([hist, x], axis=0), gpre_ref[...])
    n = a.shape[0]
    n_bufs = shift_scr.shape[1]
    gd = x.shape[1] // len(POOL_WINDOWS)
    if pos0 + 1 >= max(POOL_WINDOWS):
        inv_cnt = [1.0 / w for w in POOL_WINDOWS]
    else:
        pos1 = lax.broadcasted_iota(jnp.int32, (tm, LANES), 0) + (pl.program_id(1) * tm + pos0 + 1)
        inv_cnt = [1.0 / jnp.minimum(pos1.astype(F32), float(w)) for w in POOL_WINDOWS]
    shift_scr[:, :, 0:SUBLANES, :] = jnp.zeros((shift_scr.shape[0], shift_scr.shape[1],
                                                SUBLANES, LANES), F32)

    def shifted(level, blk, v, span):
        if span % SUBLANES == 0:
            return jnp.concatenate([jnp.zeros((span, LANES), F32), v[:n - span]], axis=0)
        shift_scr[level, blk % n_bufs, pl.ds(SUBLANES, n), :] = v
        return shift_scr[level, blk % n_bufs, pl.ds(SUBLANES - span, n), :]

    ys = []
    for g, w in enumerate(POOL_WINDOWS):
        parts = []
        for blk in range(g * gd // LANES, (g + 1) * gd // LANES):
            a_b = a[:, blk * LANES:(blk + 1) * LANES]
            s, span, level = a_b, 1, 0
            while span < w:
                s = s + shifted(level, blk, s, span)
                span *= 2
                level += 1
            parts.append(s[HIST:, :] * inv_cnt[g] - a_b[HIST:, :])
        ys.append(_dot(jnp.concatenate(parts, axis=1).astype(BF16), wg_ref[g]))
    y = jnp.concatenate(ys, axis=1) * scale_ref[...]
    return x + _rmsnorm(y, gpost_ref[...])


def _ffn_mix(x, hist, gpre_ref, wup_ref, cw_ref, cb_ref, wdn_ref, gpost_ref, act_scr, conv_scr):
    d_ff = wdn_ref.shape[0]
    tm = x.shape[0]
    hn = _rmsnorm(jnp.concatenate([hist, x], axis=0), gpre_ref[...]).astype(BF16)

    def conv(col0, buf):
        u = _dot(hn, wup_ref[:, col0:col0 + FF_CHUNK])
        parts = []
        for blk in range(FF_CHUNK // LANES):
            sl = slice(col0 + blk * LANES, col0 + (blk + 1) * LANES)
            u_b = u[:, blk * LANES:(blk + 1) * LANES]
            conv_scr[buf, blk] = u_b
            parts.append(cb_ref[:, sl] + cw_ref[2:3, sl] * u_b[HIST:, :]
                         + cw_ref[1:2, sl] * conv_scr[buf, blk, pl.ds(HIST - 1, tm), :]
                         + cw_ref[0:1, sl] * conv_scr[buf, blk, pl.ds(HIST - 2, tm), :])
        return jnp.concatenate(parts, axis=1)

    for c in range(d_ff // FF_CHUNK):
        gate = conv(c * FF_CHUNK, (2 * c) % CONV_BUFS)
        val = conv(d_ff + c * FF_CHUNK, (2 * c + 1) % CONV_BUFS)
        act = gate * (1.0 / (1.0 + jnp.exp(-gate))) * val
        act_scr[:, c * FF_CHUNK:(c + 1) * FF_CHUNK] = act.astype(BF16)
    f = _dot(act_scr[...], wdn_ref[...])
    return x + _rmsnorm(f, gpost_ref[...])


def _ffn_kernel(h_ref, prev_ref, meta_ref, *rest, has_hist):
    (gpre_ref, wup_ref, cw_ref, cb_ref, wdn_ref, gpost_ref, out_ref, act_scr, conv_scr) = rest
    out_ref[0] = _ffn_mix(h_ref[0], _history(prev_ref, meta_ref, has_hist),
                          gpre_ref, wup_ref, cw_ref, cb_ref, wdn_ref, gpost_ref, act_scr, conv_scr)


def _pool_ffn_kernel(h_ref, prev_ref, meta_ref, meta_mid_ref, pgpre_ref, wg_ref, scale_ref,
                     pgpost_ref, *rest, pos0, has_hist):
    (gpre_ref, wup_ref, cw_ref, cb_ref, wdn_ref, gpost_ref,
     out_ref, mid_ref, act_scr, conv_scr, shift_scr, hist_scr) = rest
    mid = _pool_mix(h_ref[0], _history(prev_ref, meta_ref, has_hist),
                    pgpre_ref, wg_ref, scale_ref, pgpost_ref, shift_scr, pos0)
    mid_ref[0, 0] = mid[:HIST]
    if has_hist:
        @pl.when(pl.program_id(1) == 0)
        def _():
            hist_scr[...] = meta_mid_ref[0]

        hist = hist_scr[...]
        hist_scr[...] = mid[mid.shape[0] - HIST:]
    else:
        hist = jnp.zeros(hist_scr.shape, F32)
    out_ref[0] = _ffn_mix(mid, hist, gpre_ref, wup_ref, cw_ref, cb_ref, wdn_ref, gpost_ref,
                          act_scr, conv_scr)


def _layer_spec(w, layer):
    if w.ndim == 2:
        return _const_spec(w.shape)
    return pl.BlockSpec((None,) + w.shape[1:], lambda *_: (layer, 0, 0),
                        pipeline_mode=pl.Buffered(1))


def _ffn_specs(tm, layer, ffn_w):
    d_ff = ffn_w[4].shape[-2]
    assert d_ff % FF_CHUNK == 0
    scratch = [pltpu.VMEM((tm, d_ff), BF16),
               pltpu.VMEM((CONV_BUFS, FF_CHUNK // LANES, HIST + tm, LANES), F32)]
    return [_layer_spec(w, layer) for w in ffn_w], scratch


def _conv_ffn(h, meta_h, tm, has_hist, layer, ffn_w):
    B, L, D = h.shape
    row = lambda b, t: (b, t, 0)
    prev, meta = _history_specs(tm, D)
    w_specs, scratch = _ffn_specs(tm, layer, ffn_w)
    return pl.pallas_call(
        functools.partial(_ffn_kernel, has_hist=has_hist),
        grid=(B, L // tm),
        in_specs=[pl.BlockSpec((1, tm, D), row), prev, meta] + w_specs,
        out_specs=pl.BlockSpec((1, tm, D), row),
        out_shape=jax.ShapeDtypeStruct(h.shape, h.dtype),
        scratch_shapes=scratch,
        compiler_params=_params(),
        name="conv_ffn_%d" % tm,
    )(h, h, meta_h, *ffn_w)


def _pool_ffn(h, meta_h, meta_mid, tm, pos0, has_hist, layer, pool_w, ffn_w):
    B, L, D = h.shape
    row = lambda b, t: (b, t, 0)
    prev, meta = _history_specs(tm, D)
    w_specs, scratch = _ffn_specs(tm, layer, ffn_w)
    return pl.pallas_call(
        functools.partial(_pool_ffn_kernel, pos0=pos0, has_hist=has_hist),
        grid=(B, L // tm),
        in_specs=([pl.BlockSpec((1, tm, D), row), prev, meta, meta]
                  + [_const_spec(w.shape) for w in pool_w] + w_specs),
        out_specs=[pl.BlockSpec((1, tm, D), row),
                   pl.BlockSpec((1, 1, HIST, D), lambda b, t: (b, t, 0, 0))],
        out_shape=[jax.ShapeDtypeStruct(h.shape, h.dtype),
                   jax.ShapeDtypeStruct((B, L // tm, HIST, D), h.dtype)],
        scratch_shapes=scratch + [
            pltpu.VMEM((POOL_SHIFT_LEVELS, POOL_BUFS, SUBLANES + HIST + tm, LANES), F32),
            pltpu.VMEM((HIST, D), F32),
        ],
        compiler_params=_params(),
        name="pool_ffn_%d" % tm,
    )(h, h, meta_h, meta_mid, *pool_w, *ffn_w)


def _rope_tables(pos):
    n = pos.shape[0]
    inv = 1.0 / (ROPE_THETA ** (jnp.arange(0, QK_ROPE, 2, dtype=F32) / QK_ROPE))
    ang = pos.astype(F32)[:, None] * inv[None, :]
    cos, sin = jnp.cos(ang), jnp.sin(ang)
    z_nope = jnp.zeros((n, QK_NOPE), F32)
    z_pad = jnp.zeros((n, SLOT - QK_HEAD), F32)
    c_k = jnp.concatenate([z_nope, cos, cos, z_pad], axis=1)
    s_k = jnp.concatenate([z_nope, sin, sin, z_pad], axis=1)
    qs = (QK_HEAD ** -0.5) * math.log2(math.e)
    c_q = jnp.concatenate([jnp.ones((n, QK_NOPE), F32), cos, cos, z_pad], axis=1) * qs
    return c_q, s_k * qs, c_k, s_k


def _rot_half(w):
    half = QK_ROPE // 2
    return jnp.concatenate([-w[..., half:], w[..., :half]], axis=-1)


def _mla_weights(w_dqkv, w_uq, w_ukv):
    d = w_dqkv.shape[0]
    o = Q_RANK + KV_RANK
    wkr = w_dqkv[:, o:]
    z = lambda *s: jnp.zeros(s, F32)
    pad = SLOT - QK_HEAD
    wa = jnp.concatenate([
        w_dqkv[:, :o],
        z(d, QK_NOPE), wkr, z(d, pad),
        z(d, QK_NOPE), _rot_half(wkr), z(d, pad)], axis=1).astype(BF16)
    wq = w_uq.reshape(Q_RANK, N_HEADS, QK_HEAD)
    nope, rope = wq[..., :QK_NOPE], wq[..., QK_NOPE:]
    wqm = jnp.concatenate([nope, rope, z(Q_RANK, N_HEADS, pad)], axis=-1)
    wqr = _rot_half(rope).reshape(Q_RANK, N_HEADS * QK_ROPE).astype(BF16)
    wkv = w_ukv.reshape(KV_RANK, N_HEADS, QK_NOPE + V_HEAD)
    wuk = jnp.concatenate([wkv[..., :QK_NOPE], z(KV_RANK, N_HEADS, SLOT - QK_NOPE)], axis=-1)
    wuvt = jnp.concatenate([jnp.transpose(wkv[..., QK_NOPE:], (1, 2, 0)),
                            z(N_HEADS, V_ROWS - V_HEAD, KV_RANK)], axis=1)
    wuvt = wuvt.reshape(N_HEADS * V_ROWS, KV_RANK).astype(BF16)
    flat = lambda w: w.reshape(w.shape[0], N_HEADS * SLOT).astype(BF16)
    return wa, flat(wqm), wqr, flat(wuk), wuvt


def kernel(x, meta_tokens, norm_mix_pre, norm_mix_post, norm_ffn_pre, norm_ffn_post,
           mla_w_dqkv, mla_q_norm, mla_w_uq, mla_kv_norm, mla_w_ukv, mla_w_o,
           pool_w_group, pool_scale, ffn_w_up, ffn_conv_w, ffn_conv_b, ffn_w_down):
    B, S, D = x.shape
    depth = norm_mix_pre.shape[0]
    assert meta_tokens.shape[0] == N_META == HIST
    assert S % FFN_TILE == 0 and S % ATT_TILE == 0
    h = x
    hm = jnp.concatenate([meta_tokens.astype(x.dtype),
                          jnp.zeros((META_ROWS - N_META, D), x.dtype)], axis=0)[None]
    tabs_m = _rope_tables(jnp.arange(META_ROWS))
    tabs_t = _rope_tables(N_META + jnp.arange(S))
    row = lambda v: v.reshape(1, -1)
    w_up_all, w_down_all = ffn_w_up.astype(BF16), ffn_w_down.astype(BF16)
    for i in range(depth):
        j = i // 2
        g_pre, g_post = row(norm_mix_pre[i]), row(norm_mix_post[i])
        ffn_w = (row(norm_ffn_pre[i]), w_up_all, ffn_conv_w, row(ffn_conv_b[i]), w_down_all,
                 row(norm_ffn_post[i]))
        if i % 2 == 0:
            w = _mla_weights(mla_w_dqkv[j], mla_w_uq[j], mla_w_ukv[j])
            qn, kvn = row(mla_q_norm[j]), row(mla_kv_norm[j])
            wo = mla_w_o[j].astype(BF16)
            qm, km, vtm = _mla_front(hm, META_ROWS, g_pre, w[0], qn, kvn, *w[1:], tabs_m)
            q, k, vt = _mla_front(h, ATT_TILE, g_pre, w[0], qn, kvn, *w[1:], tabs_t)
            hm = _attention(hm, qm, km, vtm, km, vtm, wo, g_post, tq=META_ROWS,
                            has_meta=False, split_diag=False)
            h = _attention(h, q, k, vt, km, vtm, wo, g_post, tq=ATT_TILE,
                           has_meta=True, split_diag=True)
            hm_new = _conv_ffn(hm, hm, META_ROWS, False, i, ffn_w)
            h = _conv_ffn(h, hm, FFN_TILE, True, i, ffn_w)
        else:
            pool_w = (g_pre, pool_w_group[j].astype(BF16), row(pool_scale[j]), g_post)
            hm_new, hm_mid = _pool_ffn(hm, hm, hm, META_ROWS, 0, False, i, pool_w, ffn_w)
            h, _ = _pool_ffn(h, hm, hm_mid[:, 0], FFN_TILE, N_META, True, i, pool_w, ffn_w)
        hm = hm_new
    return h
```

```python
import functools
import math

import jax
import jax.numpy as jnp
from jax import lax
from jax.experimental import pallas as pl
from jax.experimental.pallas import tpu as pltpu

N_META = 16
N_HEADS = 16
QK_NOPE = 64
QK_ROPE = 32
QK_HEAD = QK_NOPE + QK_ROPE
V_HEAD = 64
Q_RANK = 384
KV_RANK = 256
ROPE_THETA = 10000.0
POOL_WINDOWS = (2, 4, 8, 16)
POOL_SHIFT_LEVELS = 3
NORM_EPS = 1e-6

LANES = 128
SUBLANES = 8
BF16_ROWS = 16
SLOT = LANES
V_ROWS = V_HEAD + BF16_ROWS
VMEM_LIMIT = 56 * 1024 * 1024

META_ROWS = 128
HIST = 16
FFN_TILE = 1024
POOL_BUFS = 2
ATT_TILE = 512
FF_CHUNK = 256
CONV_BUFS = 4
AHEAD = 4
AHEAD_FULL = 2
NEG_BIG = float("-inf")

BF16 = jnp.bfloat16
F32 = jnp.float32
NT = (((1,), (1,)), ((), ()))
TN = (((0,), (0,)), ((), ()))


def _rmsnorm(x, g):
    ms = jnp.mean(x * x, axis=-1, keepdims=True)
    return x * lax.rsqrt(ms + NORM_EPS) * g


def _dot(a, b):
    return jnp.dot(a, b, preferred_element_type=F32)


def _const_spec(shape):
    nd = len(shape)
    return pl.BlockSpec(shape, lambda *_: (0,) * nd, pipeline_mode=pl.Buffered(1))


def _params():
    return pltpu.CompilerParams(
        dimension_semantics=("arbitrary", "arbitrary"),
        vmem_limit_bytes=VMEM_LIMIT,
    )


def _history_specs(tm, d):
    per_tile = tm // HIST
    prev = pl.BlockSpec((1, HIST, d), lambda b, t: (b, jnp.maximum(t * per_tile - 1, 0), 0))
    meta = pl.BlockSpec((1, HIST, d), lambda b, t: (0, 0, 0))
    return prev, meta


def _history(prev_ref, meta_ref, has_hist):
    if has_hist:
        return jnp.where(pl.program_id(1) == 0, meta_ref[0], prev_ref[0])
    return jnp.zeros(meta_ref.shape[1:], F32)


def _mla_front_kernel(h_ref, g_ref, wa_ref, qn_ref, kvn_ref, wqm_ref, wqr_ref,
                      wuk_ref, wuvt_ref, cq_ref, sq_ref, ck_ref, sk_ref,
                      q_out, k_out, vt_out):
    tm = h_ref.shape[1]
    a = _rmsnorm(h_ref[0], g_ref[...]).astype(BF16)
    y = _dot(a, wa_ref[...])
    c_q = _rmsnorm(y[:, :Q_RANK], qn_ref[...]).astype(BF16)
    c_kv = _rmsnorm(y[:, Q_RANK:Q_RANK + KV_RANK], kvn_ref[...]).astype(BF16)
    o = Q_RANK + KV_RANK
    k_rope = y[:, o:o + SLOT] * ck_ref[...] + y[:, o + SLOT:o + 2 * SLOT] * sk_ref[...]
    cq2 = jnp.concatenate([cq_ref[...], cq_ref[...]], axis=1)
    sq2 = jnp.concatenate([sq_ref[...], sq_ref[...]], axis=1)
    kr2 = jnp.concatenate([k_rope, k_rope], axis=1)
    rot_all = _dot(c_q, wqr_ref[...])
    heads_per_blk = SLOT // QK_ROPE

    def rot_slot(hd):
        blk = rot_all[:, SLOT * (hd // heads_per_blk):SLOT * (hd // heads_per_blk + 1)]
        shift = (QK_NOPE - QK_ROPE * (hd % heads_per_blk)) % SLOT
        return blk if shift == 0 else pltpu.roll(blk, shift, axis=1)

    for hp in range(N_HEADS // 2):
        sl = slice(2 * SLOT * hp, 2 * SLOT * (hp + 1))
        rot = jnp.concatenate([rot_slot(2 * hp), rot_slot(2 * hp + 1)], axis=1)
        q = _dot(c_q, wqm_ref[:, sl]) * cq2 + rot * sq2
        k = _dot(c_kv, wuk_ref[:, sl]) + kr2
        for e in range(2):
            q_out[0, 2 * hp + e] = q[:, e * SLOT:(e + 1) * SLOT].astype(BF16)
            k_out[0, 2 * hp + e] = k[:, e * SLOT:(e + 1) * SLOT].astype(BF16)
    vt_all = lax.dot_general(wuvt_ref[...], c_kv, NT, preferred_element_type=F32)
    ones_row = (lax.broadcasted_iota(jnp.int32, (V_ROWS, tm), 0) == V_HEAD).astype(F32)
    for hd in range(N_HEADS):
        vt = vt_all[V_ROWS * hd:V_ROWS * (hd + 1), :] + ones_row
        vt_out[0, hd, 0] = vt.astype(BF16)


def _mla_front(h, tm, g, wa, qn, kvn, wqm, wqr, wuk, wuvt, tabs):
    B, L, D = h.shape
    tab = pl.BlockSpec((tm, SLOT), lambda b, t: (t, 0))
    head_out = pl.BlockSpec((1, N_HEADS, tm, SLOT), lambda b, t: (b, 0, t, 0))
    qk_sds = jax.ShapeDtypeStruct((B, N_HEADS, L, SLOT), BF16)
    vt_sds = jax.ShapeDtypeStruct((B, N_HEADS, L // tm, V_ROWS, tm), BF16)
    return pl.pallas_call(
        _mla_front_kernel,
        grid=(B, L // tm),
        in_specs=[
            pl.BlockSpec((1, tm, D), lambda b, t: (b, t, 0)),
            _const_spec(g.shape), _const_spec(wa.shape), _const_spec(qn.shape),
            _const_spec(kvn.shape), _const_spec(wqm.shape), _const_spec(wqr.shape),
            _const_spec(wuk.shape), _const_spec(wuvt.shape),
            tab, tab, tab, tab,
        ],
        out_specs=[head_out, head_out,
                   pl.BlockSpec((1, N_HEADS, 1, V_ROWS, tm), lambda b, t: (b, 0, t, 0, 0))],
        out_shape=[qk_sds, qk_sds, vt_sds],
        compiler_params=_params(),
        name="mla_front_%d" % tm,
    )(h, g, wa, qn, kvn, wqm, wqr, wuk, wuvt, *tabs)


def _attn_kernel(q_ref, k_ref, vt_ref, km_ref, vtm_ref, h_ref, wo_ref, g_ref, out_ref,
                 m_scr, acc_scr, ot_scr, *, tq, has_meta, split_diag):
    j = pl.program_id(1)

    def scores(keys, hd):
        return lax.dot_general(keys, q_ref[0, hd], NT, preferred_element_type=F32)

    kd = tq // 2 if split_diag else tq
    diag0 = pl.multiple_of(j * tq, tq)
    causal_a = (lax.broadcasted_iota(jnp.int32, (kd, tq), 0)
                <= lax.broadcasted_iota(jnp.int32, (kd, tq), 1))
    causal_b = causal_a[:, :kd]

    def first_scores(hd):
        s_a = jnp.where(causal_a, scores(k_ref[0, hd, pl.ds(diag0, kd), :], hd), NEG_BIG)
        s_b = s_meta = None
        if split_diag:
            s_b = lax.dot_general(k_ref[0, hd, pl.ds(diag0 + kd, kd), :], q_ref[0, hd, kd:, :],
                                  NT, preferred_element_type=F32)
            s_b = jnp.where(causal_b, s_b, NEG_BIG)
        if has_meta:
            s_meta = scores(km_ref[0, hd, :N_META, :], hd)
        return s_a, s_b, s_meta

    def first(hd, s_all):
        s_a, s_b, s_meta = s_all
        m = jnp.max(s_a, axis=0, keepdims=True)
        if has_meta:
            m = jnp.maximum(m, jnp.max(s_meta, axis=0, keepdims=True))
        if split_diag:
            m_hi = jnp.maximum(m[:, kd:], jnp.max(s_b, axis=0, keepdims=True))
            m = jnp.concatenate([m[:, :kd], m_hi], axis=1)
        vt = vt_ref[0, hd, j]
        acc = _dot(vt[:, :kd], jnp.exp2(s_a - m).astype(BF16))
        if has_meta:
            p_meta = jnp.concatenate([jnp.exp2(s_meta - m).astype(BF16),
                                      jnp.zeros((META_ROWS - N_META, tq), BF16)], axis=0)
            acc = acc + _dot(vtm_ref[0, hd, 0], p_meta)
        if split_diag:
            acc_hi = acc[:, kd:] + _dot(vt[:, kd:], jnp.exp2(s_b - m_hi).astype(BF16))
            acc = jnp.concatenate([acc[:, :kd], acc_hi], axis=1)
        acc_scr[hd] = acc
        m_scr[hd] = m

    def update(hd, s, vt):
        m_old = m_scr[hd]
        m_new = jnp.maximum(m_old, jnp.max(s, axis=0, keepdims=True))
        alpha = jnp.exp2(m_old - m_new)
        p = jnp.exp2(s - m_new).astype(BF16)
        acc_scr[hd] = alpha * acc_scr[hd] + _dot(vt, p)
        m_scr[hd] = m_new

    def all_heads(score_fn, step_fn, ahead):
        pending = [score_fn(hd) for hd in range(ahead)]
        for hd in range(N_HEADS):
            if hd + ahead < N_HEADS:
                pending.append(score_fn(hd + ahead))
            step_fn(hd, pending.pop(0))

    all_heads(first_scores, first, AHEAD)

    def full_chunk(c, carry):
        k0 = pl.multiple_of(c * tq, tq)
        all_heads(lambda hd: scores(k_ref[0, hd, pl.ds(k0, tq), :], hd),
                  lambda hd, s: update(hd, s, vt_ref[0, hd, c]), AHEAD_FULL)
        return carry

    lax.fori_loop(0, j, full_chunk, 0)

    for hd in range(N_HEADS):
        acc = acc_scr[hd]
        o = acc[:V_HEAD] * (1.0 / acc[V_HEAD:V_HEAD + 1])
        ot_scr[V_HEAD * hd:V_HEAD * (hd + 1), :] = o.astype(BF16)
    mix = lax.dot_general(ot_scr[...], wo_ref[...], TN, preferred_element_type=F32)
    out_ref[0] = h_ref[0] + _rmsnorm(mix, g_ref[...])


def _attention(h, q, k, vt, k_meta, vt_meta, wo, g, *, tq, has_meta, split_diag):
    B, L, D = h.shape
    row = lambda b, j: (b, j, 0)
    whole = lambda a: pl.BlockSpec((1,) + a.shape[1:], lambda b, j: (b,) + (0,) * (a.ndim - 1))
    shared = lambda a: pl.BlockSpec((1,) + a.shape[1:], lambda b, j: (0,) * a.ndim)
    return pl.pallas_call(
        functools.partial(_attn_kernel, tq=tq, has_meta=has_meta, split_diag=split_diag),
        grid=(B, L // tq),
        in_specs=[
            pl.BlockSpec((1, N_HEADS, tq, SLOT), lambda b, j: (b, 0, j, 0)),
            whole(k), whole(vt), shared(k_meta), shared(vt_meta),
            pl.BlockSpec((1, tq, D), row),
            _const_spec(wo.shape), _const_spec(g.shape),
        ],
        out_specs=pl.BlockSpec((1, tq, D), row),
        out_shape=jax.ShapeDtypeStruct(h.shape, h.dtype),
        scratch_shapes=[
            pltpu.VMEM((N_HEADS, 1, tq), F32),
            pltpu.VMEM((N_HEADS, V_ROWS, tq), F32),
            pltpu.VMEM((N_HEADS * V_HEAD, tq), BF16),
        ],
        compiler_params=_params(),
        name="mla_attn_%d" % tq,
    )(q, k, vt, k_meta, vt_meta, h, wo, g)


def _pool_mix(x, hist, gpre_ref, wg_ref, scale_ref, gpost_ref, shift_scr, pos0):
    tm = x.shape[0]
    a = _rmsnorm(jnp.concatenate([hist, x], axis=0), gpre_ref[...])
    n = a.shape[0]
    n_bufs = shift_scr.shape[1]
    gd = x.shape[1] // len(POOL_WINDOWS)
    if pos0 + 1 >= max(POOL_WINDOWS):
        inv_cnt = [1.0 / w for w in POOL_WINDOWS]
    else:
        pos1 = lax.broadcasted_iota(jnp.int32, (tm, LANES), 0) + (pl.program_id(1) * tm + pos0 + 1)
        inv_cnt = [1.0 / jnp.minimum(pos1.astype(F32), float(w)) for w in POOL_WINDOWS]
    shift_scr[:, :, 0:SUBLANES, :] = jnp.zeros((shift_scr.shape[0], shift_scr.shape[1],
                                                SUBLANES, LANES), F32)

    def shifted(level, blk, v, span):
        if span % SUBLANES == 0:
            return jnp.concatenate([jnp.zeros((span, LANES), F32), v[:n - span]], axis=0)
        shift_scr[level, blk % n_bufs, pl.ds(SUBLANES, n), :] = v
        return shift_scr[level, blk % n_bufs, pl.ds(SUBLANES - span, n), :]

    ys = []
    for g, w in enumerate(POOL_WINDOWS):
        parts = []
        for blk in range(g * gd // LANES, (g + 1) * gd // LANES):
            a_b = a[:, blk * LANES:(blk + 1) * LANES]
            s, span, level = a_b, 1, 0
            while span < w:
                s = s + shifted(level, blk, s, span)
                span *= 2
                level += 1
            parts.append(s[HIST:, :] * inv_cnt[g] - a_b[HIST:, :])
        ys.append(_dot(jnp.concatenate(parts, axis=1).astype(BF16), wg_ref[g]))
    y = jnp.concatenate(ys, axis=1) * scale_ref[...]
    return x + _rmsnorm(y, gpost_ref[...])


def _ffn_mix(x, hist, gpre_ref, wup_ref, cw_ref, cb_ref, wdn_ref, gpost_ref, act_scr, conv_scr):
    d_ff = wdn_ref.shape[0]
    tm = x.shape[0]
    hn = _rmsnorm(jnp.concatenate([hist, x], axis=0), gpre_ref[...]).astype(BF16)

    def conv(col0, buf):
        u = _dot(hn, wup_ref[:, col0:col0 + FF_CHUNK])
        parts = []
        for blk in range(FF_CHUNK // LANES):
            sl = slice(col0 + blk * LANES, col0 + (blk + 1) * LANES)
            u_b = u[:, blk * LANES:(blk + 1) * LANES]
            conv_scr[buf, blk] = u_b
            parts.append(cb_ref[:, sl] + cw_ref[2:3, sl] * u_b[HIST:, :]
                         + cw_ref[1:2, sl] * conv_scr[buf, blk, pl.ds(HIST - 1, tm), :]
                         + cw_ref[0:1, sl] * conv_scr[buf, blk, pl.ds(HIST - 2, tm), :])
        return jnp.concatenate(parts, axis=1)

    for c in range(d_ff // FF_CHUNK):
        gate = conv(c * FF_CHUNK, (2 * c) % CONV_BUFS)
        val = conv(d_ff + c * FF_CHUNK, (2 * c + 1) % CONV_BUFS)
        act = gate * (1.0 / (1.0 + jnp.exp(-gate))) * val
        act_scr[:, c * FF_CHUNK:(c + 1) * FF_CHUNK] = act.astype(BF16)
    f = _dot(act_scr[...], wdn_ref[...])
    return x + _rmsnorm(f, gpost_ref[...])


def _ffn_kernel(h_ref, prev_ref, meta_ref, *rest, has_hist):
    (gpre_ref, wup_ref, cw_ref, cb_ref, wdn_ref, gpost_ref, out_ref, act_scr, conv_scr) = rest
    out_ref[0] = _ffn_mix(h_ref[0], _history(prev_ref, meta_ref, has_hist),
                          gpre_ref, wup_ref, cw_ref, cb_ref, wdn_ref, gpost_ref, act_scr, conv_scr)


def _pool_ffn_kernel(h_ref, prev_ref, meta_ref, meta_mid_ref, pgpre_ref, wg_ref, scale_ref,
                     pgpost_ref, *rest, pos0, has_hist):
    (gpre_ref, wup_ref, cw_ref, cb_ref, wdn_ref, gpost_ref,
     out_ref, mid_ref, act_scr, conv_scr, shift_scr, hist_scr) = rest
    mid = _pool_mix(h_ref[0], _history(prev_ref, meta_ref, has_hist),
                    pgpre_ref, wg_ref, scale_ref, pgpost_ref, shift_scr, pos0)
    mid_ref[0, 0] = mid[:HIST]
    if has_hist:
        @pl.when(pl.program_id(1) == 0)
        def _():
            hist_scr[...] = meta_mid_ref[0]

        hist = hist_scr[...]
        hist_scr[...] = mid[mid.shape[0] - HIST:]
    else:
        hist = jnp.zeros(hist_scr.shape, F32)
    out_ref[0] = _ffn_mix(mid, hist, gpre_ref, wup_ref, cw_ref, cb_ref, wdn_ref, gpost_ref,
                          act_scr, conv_scr)


def _layer_spec(w, layer):
    if w.ndim == 2:
        return _const_spec(w.shape)
    return pl.BlockSpec((None,) + w.shape[1:], lambda *_: (layer, 0, 0),
                        pipeline_mode=pl.Buffered(1))


def _ffn_specs(tm, layer, ffn_w):
    d_ff = ffn_w[4].shape[-2]
    assert d_ff % FF_CHUNK == 0
    scratch = [pltpu.VMEM((tm, d_ff), BF16),
               pltpu.VMEM((CONV_BUFS, FF_CHUNK // LANES, HIST + tm, LANES), F32)]
    return [_layer_spec(w, layer) for w in ffn_w], scratch


def _conv_ffn(h, meta_h, tm, has_hist, layer, ffn_w):
    B, L, D = h.shape
    row = lambda b, t: (b, t, 0)
    prev, meta = _history_specs(tm, D)
    w_specs, scratch = _ffn_specs(tm, layer, ffn_w)
    return pl.pallas_call(
        functools.partial(_ffn_kernel, has_hist=has_hist),
        grid=(B, L // tm),
        in_specs=[pl.BlockSpec((1, tm, D), row), prev, meta] + w_specs,
        out_specs=pl.BlockSpec((1, tm, D), row),
        out_shape=jax.ShapeDtypeStruct(h.shape, h.dtype),
        scratch_shapes=scratch,
        compiler_params=_params(),
        name="conv_ffn_%d" % tm,
    )(h, h, meta_h, *ffn_w)


def _pool_ffn(h, meta_h, meta_mid, tm, pos0, has_hist, layer, pool_w, ffn_w):
    B, L, D = h.shape
    row = lambda b, t: (b, t, 0)
    prev, meta = _history_specs(tm, D)
    w_specs, scratch = _ffn_specs(tm, layer, ffn_w)
    return pl.pallas_call(
        functools.partial(_pool_ffn_kernel, pos0=pos0, has_hist=has_hist),
        grid=(B, L // tm),
        in_specs=([pl.BlockSpec((1, tm, D), row), prev, meta, meta]
                  + [_const_spec(w.shape) for w in pool_w] + w_specs),
        out_specs=[pl.BlockSpec((1, tm, D), row),
                   pl.BlockSpec((1, 1, HIST, D), lambda b, t: (b, t, 0, 0))],
        out_shape=[jax.ShapeDtypeStruct(h.shape, h.dtype),
                   jax.ShapeDtypeStruct((B, L // tm, HIST, D), h.dtype)],
        scratch_shapes=scratch + [
            pltpu.VMEM((POOL_SHIFT_LEVELS, POOL_BUFS, SUBLANES + HIST + tm, LANES), F32),
            pltpu.VMEM((HIST, D), F32),
        ],
        compiler_params=_params(),
        name="pool_ffn_%d" % tm,
    )(h, h, meta_h, meta_mid, *pool_w, *ffn_w)


def _rope_tables(pos):
    n = pos.shape[0]
    inv = 1.0 / (ROPE_THETA ** (jnp.arange(0, QK_ROPE, 2, dtype=F32) / QK_ROPE))
    ang = pos.astype(F32)[:, None] * inv[None, :]
    cos, sin = jnp.cos(ang), jnp.sin(ang)
    z_nope = jnp.zeros((n, QK_NOPE), F32)
    z_pad = jnp.zeros((n, SLOT - QK_HEAD), F32)
    c_k = jnp.concatenate([z_nope, cos, cos, z_pad], axis=1)
    s_k = jnp.concatenate([z_nope, sin, sin, z_pad], axis=1)
    qs = (QK_HEAD ** -0.5) * math.log2(math.e)
    c_q = jnp.concatenate([jnp.ones((n, QK_NOPE), F32), cos, cos, z_pad], axis=1) * qs
    return c_q, s_k * qs, c_k, s_k


def _rot_half(w):
    half = QK_ROPE // 2
    return jnp.concatenate([-w[..., half:], w[..., :half]], axis=-1)


def _mla_weights(w_dqkv, w_uq, w_ukv):
    d = w_dqkv.shape[0]
    o = Q_RANK + KV_RANK
    wkr = w_dqkv[:, o:]
    z = lambda *s: jnp.zeros(s, F32)
    pad = SLOT - QK_HEAD
    wa = jnp.concatenate([
        w_dqkv[:, :o],
        z(d, QK_NOPE), wkr, z(d, pad),
        z(d, QK_NOPE), _rot_half(wkr), z(d, pad)], axis=1).astype(BF16)
    wq = w_uq.reshape(Q_RANK, N_HEADS, QK_HEAD)
    nope, rope = wq[..., :QK_NOPE], wq[..., QK_NOPE:]
    wqm = jnp.concatenate([nope, rope, z(Q_RANK, N_HEADS, pad)], axis=-1)
    wqr = _rot_half(rope).reshape(Q_RANK, N_HEADS * QK_ROPE).astype(BF16)
    wkv = w_ukv.reshape(KV_RANK, N_HEADS, QK_NOPE + V_HEAD)
    wuk = jnp.concatenate([wkv[..., :QK_NOPE], z(KV_RANK, N_HEADS, SLOT - QK_NOPE)], axis=-1)
    wuvt = jnp.concatenate([jnp.transpose(wkv[..., QK_NOPE:], (1, 2, 0)),
                            z(N_HEADS, V_ROWS - V_HEAD, KV_RANK)], axis=1)
    wuvt = wuvt.reshape(N_HEADS * V_ROWS, KV_RANK).astype(BF16)
    flat = lambda w: w.reshape(w.shape[0], N_HEADS * SLOT).astype(BF16)
    return wa, flat(wqm), wqr, flat(wuk), wuvt


def kernel(x, meta_tokens, norm_mix_pre, norm_mix_post, norm_ffn_pre, norm_ffn_post,
           mla_w_dqkv, mla_q_norm, mla_w_uq, mla_kv_norm, mla_w_ukv, mla_w_o,
           pool_w_group, pool_scale, ffn_w_up, ffn_conv_w, ffn_conv_b, ffn_w_down):
    B, S, D = x.shape
    depth = norm_mix_pre.shape[0]
    assert meta_tokens.shape[0] == N_META == HIST
    assert S % FFN_TILE == 0 and S % ATT_TILE == 0
    h = x
    hm = jnp.concatenate([meta_tokens.astype(x.dtype),
                          jnp.zeros((META_ROWS - N_META, D), x.dtype)], axis=0)[None]
    tabs_m = _rope_tables(jnp.arange(META_ROWS))
    tabs_t = _rope_tables(N_META + jnp.arange(S))
    row = lambda v: v.reshape(1, -1)
    w_up_all, w_down_all = ffn_w_up.astype(BF16), ffn_w_down.astype(BF16)
    for i in range(depth):
        j = i // 2
        g_pre, g_post = row(norm_mix_pre[i]), row(norm_mix_post[i])
        ffn_w = (row(norm_ffn_pre[i]), w_up_all, ffn_conv_w, row(ffn_conv_b[i]), w_down_all,
                 row(norm_ffn_post[i]))
        if i % 2 == 0:
            w = _mla_weights(mla_w_dqkv[j], mla_w_uq[j], mla_w_ukv[j])
            qn, kvn = row(mla_q_norm[j]), row(mla_kv_norm[j])
            wo = mla_w_o[j].astype(BF16)
            qm, km, vtm = _mla_front(hm, META_ROWS, g_pre, w[0], qn, kvn, *w[1:], tabs_m)
            q, k, vt = _mla_front(h, ATT_TILE, g_pre, w[0], qn, kvn, *w[1:], tabs_t)
            hm = _attention(hm, qm, km, vtm, km, vtm, wo, g_post, tq=META_ROWS,
                            has_meta=False, split_diag=False)
            h = _attention(h, q, k, vt, km, vtm, wo, g_post, tq=ATT_TILE,
                           has_meta=True, split_diag=True)
            hm_new = _conv_ffn(hm, hm, META_ROWS, False, i, ffn_w)
            h = _conv_ffn(h, hm, FFN_TILE, True, i, ffn_w)
        else:
            pool_w = (g_pre, pool_w_group[j].astype(BF16), row(pool_scale[j]), g_post)
            hm_new, hm_mid = _pool_ffn(hm, hm, hm, META_ROWS, 0, False, i, pool_w, ffn_w)
            h, _ = _pool_ffn(h, hm, hm_mid[:, 0], FFN_TILE, N_META, True, i, pool_w, ffn_w)
        hm = hm_new
    return h
```
